```python
import jax, jax.numpy as jnp
from jax import lax
import numpy as np

D_MODEL = 1024
BATCH = 32
SEQ = 2048
DEPTH = 2

CTX_LEN = 256
GRID_W = 64
RET_HEADS = 8
RET_DK = 128
RET_DV = 256
RET_QK = RET_HEADS * RET_DK
RET_V = RET_HEADS * RET_DV
CONV_WIDTH = D_MODEL
CONV_K = 3
N_BRANCH = 2
CHUNK = 128
D_FF = 256 * (-(-(8 * D_MODEL) // (3 * 256)))
ROPE_THETA = 10000.0
EPS = 1e-6
IN_COLS = 2 * RET_QK + 2 * RET_V + 3 * CONV_WIDTH + N_BRANCH * D_MODEL
IN_SPLITS = (RET_QK, 2 * RET_QK, 2 * RET_QK + RET_V, 2 * RET_QK + 2 * RET_V,
             2 * RET_QK + 2 * RET_V + CONV_WIDTH, 2 * RET_QK + 2 * RET_V + 2 * CONV_WIDTH,
             2 * RET_QK + 2 * RET_V + 3 * CONV_WIDTH)

kernel_name = 'hybrid_retention_shortconv_dit_block'


def rms_norm(x, w):
    xf = x.astype(jnp.float32)
    y = xf * lax.rsqrt(jnp.mean(xf * xf, axis=-1, keepdims=True) + EPS)
    return (y * w.astype(jnp.float32)).astype(x.dtype)


def modulation(cvec, w, b):
    m = jax.nn.silu(cvec) @ w + b
    return m.reshape(m.shape[:-1] + (6, D_MODEL))


def modulate(h, shift, scale):
    return h * (1.0 + scale) + shift


def _rotate(x, pos):
    d = x.shape[-1]
    freqs = ROPE_THETA ** (-jnp.arange(0, d, 2, dtype=jnp.float32) / d)
    ang = pos.astype(jnp.float32)[:, None] * freqs[None, :]
    cos = jnp.cos(ang)[:, None, :]
    sin = jnp.sin(ang)[:, None, :]
    xf = x.astype(jnp.float32)
    x1, x2 = xf[..., : d // 2], xf[..., d // 2:]
    return jnp.concatenate([x1 * cos - x2 * sin, x1 * sin + x2 * cos], axis=-1).astype(x.dtype)


def rope_2d(x, rows, cols):
    h = x.shape[-1] // 2
    return jnp.concatenate([_rotate(x[..., :h], rows), _rotate(x[..., h:], cols)], axis=-1)


def retention_chunkwise(q, k, v, log_g, s0):
    bsz, t, h, dk = q.shape
    dv = v.shape[-1]
    n = t // CHUNK
    idx = jnp.arange(CHUNK, dtype=jnp.float32)
    diff = idx[:, None] - idx[None, :]
    intra = jnp.exp(jnp.where(diff[None] >= 0, diff[None] * log_g[:, None, None], -jnp.inf))
    q_dec = jnp.exp((idx[:, None] + 1.0) * log_g[None, :])
    k_dec = jnp.exp((CHUNK - 1.0 - idx)[:, None] * log_g[None, :])
    chunk_dec = jnp.exp(CHUNK * log_g)

    def to_chunks(a):
        a = a.astype(jnp.float32).reshape(bsz, n, CHUNK, h, a.shape[-1])
        return jnp.moveaxis(a, 1, 0)

    def step(s, xs):
        qc, kc, vc = xs
        scores = jnp.einsum('bihd,bjhd->bhij', qc, kc) * intra
        o = jnp.einsum('bhij,bjhv->bihv', scores, vc)
        o = o + jnp.einsum('bihd,bhdv->bihv', qc, s) * q_dec[None, :, :, None]
        s = s * chunk_dec[None, :, None, None] + jnp.einsum(
            'bjhd,bjhv->bhdv', kc * k_dec[None, :, :, None], vc)
        return s, o

    s_final, o = lax.scan(step, s0.astype(jnp.float32), (to_chunks(q), to_chunks(k), to_chunks(v)))
    o = jnp.moveaxis(o, 0, 1).reshape(bsz, t, h, dv)
    return o, s_final


def bidir_retention(q, k, v, log_g, s0_fwd, s0_bwd):
    o_f, s_f = retention_chunkwise(q, k, v, log_g[0], s0_fwd)
    o_b, s_b = retention_chunkwise(q[:, ::-1], k[:, ::-1], v[:, ::-1], log_g[1], s0_bwd)
    return o_f + o_b[:, ::-1], s_f, s_b


def retention_final_states(k, v, log_g):
    t = k.shape[1]
    pos = jnp.arange(t, dtype=jnp.float32)
    w_f = jnp.exp((t - 1.0 - pos)[:, None] * log_g[0][None, :])
    w_b = jnp.exp(pos[:, None] * log_g[1][None, :])
    kf = k.astype(jnp.float32)
    vf = v.astype(jnp.float32)
    s_f = jnp.einsum('bthd,bthv->bhdv', kf * w_f[None, :, :, None], vf)
    s_b = jnp.einsum('bthd,bthv->bhdv', kf * w_b[None, :, :, None], vf)
    return s_f, s_b


def head_group_norm(o):
    mu = jnp.mean(o, axis=-1, keepdims=True)
    var = jnp.mean(jnp.square(o - mu), axis=-1, keepdims=True)
    return (o - mu) * lax.rsqrt(var + EPS)


def short_conv(u, w):
    t = u.shape[1]
    pad = CONV_K // 2
    up = jnp.pad(u, ((0, 0), (pad, pad), (0, 0)))
    out = up[:, 0:t] * w[0]
    for j in range(1, CONV_K):
        out = out + up[:, j:j + t] * w[j]
    return out


def merge_branches(o_ret, g, cb, cc, cu, gates, conv_w, w_ret_out, w_conv_out, w_o):
    bsz, t = g.shape[:2]
    y_ret = (head_group_norm(o_ret).reshape(bsz, t, RET_V).astype(g.dtype) * jax.nn.silu(g)) @ w_ret_out
    y_conv = (cb * short_conv(cc * cu, conv_w)) @ w_conv_out
    gr, gc = jnp.split(jax.nn.sigmoid(gates), 2, axis=-1)
    return (gr * y_ret + gc * y_conv) @ w_o


def hybrid_mixer(h_lat, h_ctx, w_in, decay_logit, conv_w, w_ret_out, w_conv_out, w_o, rows, cols, ctx_out):
    log_g = jax.nn.log_sigmoid(decay_logit.astype(jnp.float32))
    bsz, t_ctx = h_ctx.shape[:2]
    k_scale = RET_DK ** -0.5
    if ctx_out:
        q_c, k_c, v_c, g_c, b_c, c_c, u_c, gate_c = jnp.split(h_ctx @ w_in, IN_SPLITS, axis=-1)
        q_c = q_c.reshape(bsz, t_ctx, RET_HEADS, RET_DK)
        k_c = k_c.reshape(bsz, t_ctx, RET_HEADS, RET_DK) * k_scale
        v_c = v_c.reshape(bsz, t_ctx, RET_HEADS, RET_DV)
        s0 = jnp.zeros((bsz, RET_HEADS, RET_DK, RET_DV), jnp.float32)
        o_c, s_fwd, s_bwd = bidir_retention(q_c, k_c, v_c, log_g, s0, s0)
        y_ctx = merge_branches(o_c, g_c, b_c, c_c, u_c, gate_c, conv_w, w_ret_out, w_conv_out, w_o)
    else:
        k_c, v_c = jnp.split(h_ctx @ w_in[:, RET_QK:2 * RET_QK + RET_V], [RET_QK], axis=-1)
        k_c = k_c.reshape(bsz, t_ctx, RET_HEADS, RET_DK) * k_scale
        v_c = v_c.reshape(bsz, t_ctx, RET_HEADS, RET_DV)
        s_fwd, s_bwd = retention_final_states(k_c, v_c, log_g)
        y_ctx = None
    t = h_lat.shape[1]
    q, k, v, g, cb, cc, cu, gates = jnp.split(h_lat @ w_in, IN_SPLITS, axis=-1)
    q = rope_2d(q.reshape(bsz, t, RET_HEADS, RET_DK), rows, cols)
    k = rope_2d(k.reshape(bsz, t, RET_HEADS, RET_DK) * k_scale, rows, cols)
    v = v.reshape(bsz, t, RET_HEADS, RET_DV)
    o_lat, _, _ = bidir_retention(q, k, v, log_g, s_fwd, s_bwd)
    y_lat = merge_branches(o_lat, g, cb, cc, cu, gates, conv_w, w_ret_out, w_conv_out, w_o)
    return y_lat, y_ctx


def swiglu(h, w_in, w_out):
    gt, up = jnp.split(h @ w_in, 2, axis=-1)
    return (jax.nn.silu(gt) * up) @ w_out


def setup_inputs(seed: int = 0) -> dict:
    key = jax.random.key(seed)
    ks = jax.random.split(key, 17)
    f32 = jnp.float32

    def nrm(k, shape, scale):
        return jax.random.normal(k, shape, f32) * scale

    decay_init = jnp.log(2.0 ** (5.0 + jnp.arange(RET_HEADS, dtype=f32)) - 1.0)
    return {
        'x': nrm(ks[0], (BATCH, SEQ, D_MODEL), 1.0),
        'c': nrm(ks[1], (BATCH, D_MODEL), 1.0),
        'ctx': nrm(ks[2], (BATCH, CTX_LEN, D_MODEL), 1.0),
        'c_ctx': nrm(ks[3], (D_MODEL,), 1.0),
        'norm_mix_w': 1.0 + nrm(ks[4], (DEPTH, D_MODEL), 0.02),
        'ada_w': nrm(ks[5], (DEPTH, D_MODEL, 6 * D_MODEL), 0.5 * D_MODEL ** -0.5),
        'ada_b': nrm(ks[6], (DEPTH, 6 * D_MODEL), 0.01),
        'w_in': nrm(ks[7], (DEPTH, D_MODEL, IN_COLS), D_MODEL ** -0.5),
        'ret_decay_logit': decay_init + nrm(ks[8], (DEPTH, 2, RET_HEADS), 0.1),
        'conv_w': nrm(ks[9], (DEPTH, CONV_K, CONV_WIDTH), CONV_K ** -0.5),
        'w_ret_out': nrm(ks[10], (DEPTH, RET_V, D_MODEL), RET_V ** -0.5),
        'w_conv_out': nrm(ks[11], (DEPTH, CONV_WIDTH, D_MODEL), CONV_WIDTH ** -0.5),
        'w_o': nrm(ks[12], (DEPTH, D_MODEL, D_MODEL), D_MODEL ** -0.5),
        'norm_ffn_w': 1.0 + nrm(ks[13], (DEPTH, D_MODEL), 0.02),
        'w_ffn_in': nrm(ks[14], (DEPTH, D_MODEL, 2 * D_FF), D_MODEL ** -0.5),
        'w_ffn_out': nrm(ks[15], (DEPTH, D_FF, D_MODEL), D_FF ** -0.5),
        'final_norm_w': 1.0 + nrm(ks[16], (D_MODEL,), 0.02),
    }


def reference(x, c, ctx, c_ctx, norm_mix_w, ada_w, ada_b, w_in, ret_decay_logit, conv_w,
              w_ret_out, w_conv_out, w_o, norm_ffn_w, w_ffn_in, w_ffn_out, final_norm_w):
    seq = x.shape[1]
    n_rows = seq // GRID_W
    rows = jnp.broadcast_to(jnp.arange(n_rows)[:, None], (n_rows, GRID_W)).reshape(-1)
    cols = jnp.broadcast_to(jnp.arange(GRID_W)[None, :], (n_rows, GRID_W)).reshape(-1)
    xc = ctx
    for l in range(DEPTH):
        last = l == DEPTH - 1
        m_lat = modulation(c, ada_w[l], ada_b[l])[:, :, None, :]
        m_ctx = modulation(c_ctx, ada_w[l], ada_b[l])
        h_lat = modulate(rms_norm(x, norm_mix_w[l]), m_lat[:, 0], m_lat[:, 1])
        h_ctx = modulate(rms_norm(xc, norm_mix_w[l]), m_ctx[0], m_ctx[1])
        y_lat, y_ctx = hybrid_mixer(h_lat, h_ctx, w_in[l], ret_decay_logit[l], conv_w[l],
                                    w_ret_out[l], w_conv_out[l], w_o[l], rows, cols, not last)
        x = x + m_lat[:, 2] * y_lat
        h = modulate(rms_norm(x, norm_ffn_w[l]), m_lat[:, 3], m_lat[:, 4])
        x = x + m_lat[:, 5] * swiglu(h, w_ffn_in[l], w_ffn_out[l])
        if not last:
            xc = xc + m_ctx[2] * y_ctx
            hc = modulate(rms_norm(xc, norm_ffn_w[l]), m_ctx[3], m_ctx[4])
            xc = xc + m_ctx[5] * swiglu(hc, w_ffn_in[l], w_ffn_out[l])
    return rms_norm(x, final_norm_w)
```

```python
import functools

import jax
import jax.numpy as jnp
from jax import lax
from jax.experimental import pallas as pl
from jax.experimental.pallas import tpu as pltpu

F32 = jnp.float32
BF16 = jnp.bfloat16

D_MODEL = 1024
HEADS = 8
DK = 128
DV = 256
QK = HEADS * DK
VW = HEADS * DV
CONV_W = D_MODEL
D_FF = 2816
GRID_W = 64
ROPE_THETA = 10000.0
EPS = 1e-6
OFF_Q, OFF_K, OFF_V, OFF_G = 0, QK, 2 * QK, 2 * QK + VW
OFF_CB = 2 * QK + 2 * VW
OFF_CC = OFF_CB + CONV_W
OFF_CU = OFF_CC + CONV_W
OFF_GT = OFF_CU + CONV_W
IN_COLS = OFF_GT + 2 * D_MODEL

RET_CHUNK = 256
MOD_ROWS = 40
VMEM_LIMIT = 56 * 1024 * 1024


def _silu(x):
    return x / (1.0 + jnp.exp(-x))


def _sigmoid(x):
    return 1.0 / (1.0 + jnp.exp(-x))


def _log_sigmoid(x):
    return jnp.minimum(x, 0.0) - jnp.log1p(jnp.exp(-jnp.abs(x)))


def _rms_norm(x, w):
    ms = jnp.mean(x * x, axis=-1, keepdims=True)
    return x * lax.rsqrt(ms + EPS) * w


def _dot(a, b):
    return jnp.dot(a, b, preferred_element_type=F32)


def _mod_kernel(c_ref, w_ref, b_ref, o_ref):
    s = _silu(c_ref[...]).astype(BF16)
    o_ref[0] = _dot(s, w_ref[0].astype(BF16)) + b_ref[0]


def _modulation(cvec, ada_w, ada_b):
    depth = ada_w.shape[0]
    nj = ada_w.shape[2] // D_MODEL
    return pl.pallas_call(
        _mod_kernel,
        grid=(depth, nj),
        in_specs=[
            pl.BlockSpec((MOD_ROWS, D_MODEL), lambda l, j: (0, 0)),
            pl.BlockSpec((1, D_MODEL, D_MODEL), lambda l, j: (l, 0, j)),
            pl.BlockSpec((1, 1, D_MODEL), lambda l, j: (l, 0, j)),
        ],
        out_specs=pl.BlockSpec((1, MOD_ROWS, D_MODEL), lambda l, j: (l, 0, j)),
        out_shape=jax.ShapeDtypeStruct((depth, MOD_ROWS, ada_w.shape[2]), F32),
        compiler_params=pltpu.CompilerParams(
            dimension_semantics=("arbitrary", "arbitrary"), vmem_limit_bytes=VMEM_LIMIT),
        name="modulation",
    )(cvec, ada_w, ada_b.reshape(depth, 1, -1))


def _inproj_kernel(x_ref, mod_ref, nw_ref, w_ref, cq_ref, sq_ref, ck_ref, sk_ref,
                   q_ref, k_ref, v_ref, sg_ref, cb_ref, u_ref, gt_ref):
    tm = x_ref.shape[1]
    y = _rms_norm(x_ref[0], nw_ref[...])
    h = (y * (1.0 + mod_ref[0, 1:2, :]) + mod_ref[0, 0:1, :]).astype(BF16)

    def proj(off, width):
        return _dot(h, w_ref[:, off:off + width])

    lane = lax.broadcasted_iota(jnp.int32, (tm, DK), 1)
    first_half = (lane % 64) < 32

    def rope(r, c_ref, s_ref, o_ref):
        c = c_ref[...]
        s = s_ref[...]
        for hd in range(HEADS):
            xh = r[:, hd * DK:(hd + 1) * DK]
            partner = jnp.where(first_half, pltpu.roll(xh, DK - 32, 1), pltpu.roll(xh, 32, 1))
            o_ref[0, :, hd * DK:(hd + 1) * DK] = (xh * c + partner * s).astype(BF16)

    rope(proj(OFF_Q, QK), cq_ref, sq_ref, q_ref)
    rope(proj(OFF_K, QK), ck_ref, sk_ref, k_ref)
    for j in range(VW // D_MODEL):
        cols = slice(j * D_MODEL, (j + 1) * D_MODEL)
        v_ref[0, :, cols] = proj(OFF_V + j * D_MODEL, D_MODEL).astype(BF16)
        sg_ref[0, :, cols] = _silu(proj(OFF_G + j * D_MODEL, D_MODEL)).astype(BF16)
        gt_ref[0, :, cols] = _sigmoid(proj(OFF_GT + j * D_MODEL, D_MODEL)).astype(BF16)
    cb_ref[0] = proj(OFF_CB, CONV_W).astype(BF16)
    u_ref[0] = (proj(OFF_CC, CONV_W) * proj(OFF_CU, CONV_W)).astype(BF16)


def _in_projection(x, mod, per_batch_mod, norm_w, w_in, tables, tm):
    bsz, t, _ = x.shape
    mod_map = (lambda b, i: (b, 0, 0)) if per_batch_mod else (lambda b, i: (0, 0, 0))
    tab_spec = pl.BlockSpec((tm, DK), lambda b, i: (i, 0))
    tok = lambda w: pl.BlockSpec((1, tm, w), lambda b, i: (b, i, 0))
    widths = (QK, QK, VW, VW, CONV_W, CONV_W, 2 * D_MODEL)
    return pl.pallas_call(
        _inproj_kernel,
        grid=(bsz, t // tm),
        in_specs=[
            tok(D_MODEL),
            pl.BlockSpec((1, 6, D_MODEL), mod_map),
            pl.BlockSpec((1, D_MODEL), lambda b, i: (0, 0)),
            pl.BlockSpec((D_MODEL, IN_COLS), lambda b, i: (0, 0)),
            tab_spec, tab_spec, tab_spec, tab_spec,
        ],
        out_specs=[tok(w) for w in widths],
        out_shape=[jax.ShapeDtypeStruct((bsz, t, w), BF16) for w in widths],
        compiler_params=pltpu.CompilerParams(
            dimension_semantics=("arbitrary", "arbitrary"), vmem_limit_bytes=VMEM_LIMIT),
        name="in_projection",
    )(x, mod, norm_w.reshape(1, D_MODEL), w_in, *tables)


def _ret_kernel(dl_ref, q_ref, k_ref, v_ref, sg_ref, s0_ref, yr_ref, sfin_ref, u_ref, ss_ref, *, n_chunks):
    c_len = RET_CHUNK
    lgf = _log_sigmoid(dl_ref[0, 0, 0:1, :])
    lgb = _log_sigmoid(dl_ref[1, 0, 0:1, :])
    ii = lax.broadcasted_iota(jnp.int32, (c_len, c_len), 0).astype(F32)
    jj = lax.broadcasted_iota(jnp.int32, (c_len, c_len), 1).astype(F32)
    diff = ii - jj
    mask = (jnp.where(diff >= 0, jnp.exp(jnp.maximum(diff, 0.0) * lgf), 0.0)
            + jnp.where(diff <= 0, jnp.exp(jnp.maximum(-diff, 0.0) * lgb), 0.0))
    ri = lax.broadcasted_iota(jnp.int32, (c_len, DK), 0).astype(F32)
    lgf_k, lgb_k = lgf[:, :DK], lgb[:, :DK]
    q_dec_f = jnp.exp((ri + 1.0) * lgf_k)
    q_dec_b = jnp.exp((c_len - ri) * lgb_k)
    k_dec_f = jnp.exp((c_len - 1.0 - ri) * lgf_k)
    k_dec_b = jnp.exp(ri * lgb_k)
    chunk_dec_f = jnp.exp(c_len * lgf)
    chunk_dec_b = jnp.exp(c_len * lgb)

    def rows(c):
        return pl.ds(pl.multiple_of(c * c_len, c_len), c_len)

    def inc_body(c, carry):
        kc = k_ref[0, rows(c), :].astype(F32)
        kk = jnp.concatenate([kc * k_dec_f, kc * k_dec_b], axis=1).astype(BF16)
        u_ref[c] = lax.dot_general(kk, v_ref[0, rows(c), :], (((0,), (0,)), ((), ())),
                                   preferred_element_type=F32)
        return carry

    lax.fori_loop(0, n_chunks, inc_body, 0)

    def fwd_body(c, s):
        ss_ref[c, 0:DK, :] = s.astype(BF16)
        return s * chunk_dec_f + u_ref[c, 0:DK, :]

    s_f = lax.fori_loop(0, n_chunks, fwd_body, s0_ref[0, 0, 0:DK, :])

    def bwd_body(i, s):
        c = n_chunks - 1 - i
        ss_ref[c, DK:2 * DK, :] = s.astype(BF16)
        return s * chunk_dec_b + u_ref[c, DK:2 * DK, :]

    s_b = lax.fori_loop(0, n_chunks, bwd_body, s0_ref[0, 0, DK:2 * DK, :])
    sfin_ref[0, 0, 0:DK, :] = s_f
    sfin_ref[0, 0, DK:2 * DK, :] = s_b

    def out_body(c, carry):
        qc = q_ref[0, rows(c), :]
        scores = lax.dot_general(qc, k_ref[0, rows(c), :], (((1,), (1,)), ((), ())),
                                 preferred_element_type=F32)
        p = (scores * mask).astype(BF16)
        qf = qc.astype(F32)
        qq = jnp.concatenate([qf * q_dec_f, qf * q_dec_b], axis=1).astype(BF16)
        o = _dot(p, v_ref[0, rows(c), :]) + _dot(qq, ss_ref[c])
        mu = jnp.mean(o, axis=-1, keepdims=True)
        oc = o - mu
        var = jnp.mean(oc * oc, axis=-1, keepdims=True)
        on = oc * lax.rsqrt(var + EPS)
        yr_ref[0, rows(c), :] = (on * sg_ref[0, rows(c), :].astype(F32)).astype(BF16)
        return carry

    lax.fori_loop(0, n_chunks, out_body, 0)


def _retention(dl, q, k, v, sg, s0):
    bsz, t, _ = q.shape
    n_chunks = t // RET_CHUNK
    tok = lambda w: pl.BlockSpec((1, t, w), lambda b, hd: (b, 0, hd))
    st = pl.BlockSpec((1, 1, 2 * DK, DV), lambda b, hd: (b, hd, 0, 0))
    return pl.pallas_call(
        functools.partial(_ret_kernel, n_chunks=n_chunks),
        grid=(bsz, HEADS),
        in_specs=[
            pl.BlockSpec((2, 1, 8, DV), lambda b, hd: (0, hd, 0, 0)),
            tok(DK), tok(DK), tok(DV), tok(DV), st,
        ],
        out_specs=[tok(DV), st],
        out_shape=[jax.ShapeDtypeStruct((bsz, t, VW), BF16),
                   jax.ShapeDtypeStruct((bsz, HEADS, 2 * DK, DV), F32)],
        scratch_shapes=[pltpu.VMEM((n_chunks, 2 * DK, DV), F32),
                        pltpu.VMEM((n_chunks, 2 * DK, DV), BF16)],
        compiler_params=pltpu.CompilerParams(
            dimension_semantics=("arbitrary", "arbitrary"), vmem_limit_bytes=VMEM_LIMIT),
        name="retention",
    )(dl, q, k, v, sg, s0)


def _conv_kernel(cb_ref, u_ref, w_ref, z_ref):
    t = u_ref.shape[1]
    u = u_ref[0].astype(F32)
    pos = lax.broadcasted_iota(jnp.int32, u.shape, 0)
    prev = jnp.where(pos == 0, 0.0, pltpu.roll(u, 1, 0))
    nxt = jnp.where(pos == t - 1, 0.0, pltpu.roll(u, t - 1, 0))
    conv = prev * w_ref[0:1, :] + u * w_ref[1:2, :] + nxt * w_ref[2:3, :]
    z_ref[0] = (cb_ref[0].astype(F32) * conv).astype(BF16)


def _short_conv(cb, u, conv_w):
    bsz, t, _ = cb.shape
    cw = 256
    tok = pl.BlockSpec((1, t, cw), lambda b, j: (b, 0, j))
    return pl.pallas_call(
        _conv_kernel,
        grid=(bsz, CONV_W // cw),
        in_specs=[tok, tok, pl.BlockSpec((3, cw), lambda b, j: (0, j))],
        out_specs=tok,
        out_shape=jax.ShapeDtypeStruct((bsz, t, CONV_W), BF16),
        compiler_params=pltpu.CompilerParams(
            dimension_semantics=("arbitrary", "arbitrary"), vmem_limit_bytes=VMEM_LIMIT),
        name="short_conv",
    )(cb, u, conv_w)


FF_CHUNKS = ((0, 1024), (1024, 2048), (2048, D_FF))


def _merge_ffn_kernel(yr_ref, z_ref, gt_ref, x_ref, mod_ref, nfw_ref, wro_ref, wco_ref, wo_ref,
                      wfi_ref, wfo_ref, fnw_ref, o_ref, *, final_norm):
    y_ret = _dot(yr_ref[0], wro_ref[...])
    y_conv = _dot(z_ref[0], wco_ref[...])
    mix = (gt_ref[0, :, 0:D_MODEL].astype(F32) * y_ret
           + gt_ref[0, :, D_MODEL:2 * D_MODEL].astype(F32) * y_conv).astype(BF16)
    x1 = x_ref[0] + mod_ref[0, 2:3, :] * _dot(mix, wo_ref[...])
    h = (_rms_norm(x1, nfw_ref[...]) * (1.0 + mod_ref[0, 4:5, :]) + mod_ref[0, 3:4, :]).astype(BF16)
    acc = None
    for c0, c1 in FF_CHUNKS:
        a = (_silu(_dot(h, wfi_ref[:, c0:c1])) * _dot(h, wfi_ref[:, D_FF + c0:D_FF + c1])).astype(BF16)
        part = _dot(a, wfo_ref[c0:c1, :])
        acc = part if acc is None else acc + part
    x2 = x1 + mod_ref[0, 5:6, :] * acc
    if final_norm:
        x2 = _rms_norm(x2, fnw_ref[...])
    o_ref[0] = x2


def _merge_ffn(yr, z, gt, x, mod, per_batch_mod, norm_ffn_w, w_ret_out, w_conv_out, w_o,
               w_ffn_in, w_ffn_out, final_norm_w, final_norm, tm):
    bsz, t, _ = x.shape
    mod_map = (lambda b, i: (b, 0, 0)) if per_batch_mod else (lambda b, i: (0, 0, 0))
    tok = lambda w: pl.BlockSpec((1, tm, w), lambda b, i: (b, i, 0))
    full = lambda a: pl.BlockSpec(a.shape, lambda b, i: (0,) * a.ndim)
    nfw = norm_ffn_w.reshape(1, D_MODEL)
    fnw = final_norm_w.reshape(1, D_MODEL)
    weights = (w_ret_out, w_conv_out, w_o, w_ffn_in, w_ffn_out)
    return pl.pallas_call(
        functools.partial(_merge_ffn_kernel, final_norm=final_norm),
        grid=(bsz, t // tm),
        in_specs=[tok(VW), tok(CONV_W), tok(2 * D_MODEL), tok(D_MODEL),
                  pl.BlockSpec((1, 6, D_MODEL), mod_map), full(nfw),
                  *[full(w) for w in weights], full(fnw)],
        out_specs=tok(D_MODEL),
        out_shape=jax.ShapeDtypeStruct((bsz, t, D_MODEL), F32),
        compiler_params=pltpu.CompilerParams(
            dimension_semantics=("arbitrary", "arbitrary"), vmem_limit_bytes=VMEM_LIMIT),
        name="merge_ffn",
    )(yr, z, gt, x, mod, nfw, *weights, fnw)


def _rope_tables(t, k_scale):
    pos = jnp.arange(t)
    rows = (pos // GRID_W).astype(F32)
    cols = (pos % GRID_W).astype(F32)
    freqs = ROPE_THETA ** (-jnp.arange(0, 64, 2, dtype=F32) / 64)
    ang = jnp.concatenate([rows[:, None] * freqs[None, :]] * 2 + [cols[:, None] * freqs[None, :]] * 2, axis=1)
    sign = jnp.tile(jnp.concatenate([-jnp.ones((32,), F32), jnp.ones((32,), F32)]), 2)
    cos, sin = jnp.cos(ang), jnp.sin(ang) * sign[None, :]
    return cos, sin, cos * k_scale, sin * k_scale


def _plain_tables(t, k_scale):
    one = jnp.ones((t, DK), F32)
    zero = jnp.zeros((t, DK), F32)
    return one, zero, one * k_scale, zero


def kernel(x, c, ctx, c_ctx, norm_mix_w, ada_w, ada_b, w_in, ret_decay_logit, conv_w, w_ret_out,
           w_conv_out, w_o, norm_ffn_w, w_ffn_in, w_ffn_out, final_norm_w):
    bsz, seq, _ = x.shape
    t_ctx = ctx.shape[1]
    depth = w_in.shape[0]
    k_scale = DK ** -0.5
    tm_lat = min(256, seq)
    tm_ctx = min(256, t_ctx)

    cvec = jnp.zeros((MOD_ROWS, D_MODEL), F32).at[:bsz].set(c).at[bsz].set(c_ctx)
    mod = _modulation(cvec, ada_w, ada_b).reshape(depth, MOD_ROWS, 6, D_MODEL)
    lat_tables = _rope_tables(seq, k_scale)
    ctx_tables = _plain_tables(t_ctx, k_scale)
    zero_state = jnp.zeros((bsz, HEADS, 2 * DK, DV), F32)

    xc = ctx
    for l in range(depth):
        last = l == depth - 1
        w_in_l = w_in[l].astype(BF16)
        wts = (w_ret_out[l].astype(BF16), w_conv_out[l].astype(BF16), w_o[l].astype(BF16),
               w_ffn_in[l].astype(BF16), w_ffn_out[l].astype(BF16))
        dl = jnp.broadcast_to(ret_decay_logit[l][:, :, None, None], (2, HEADS, 8, DV))
        mod_lat = mod[l, :bsz]
        mod_ctx = mod[l, bsz:bsz + 1]

        qc, kc, vc, sgc, cbc, uc, gtc = _in_projection(xc, mod_ctx, False, norm_mix_w[l], w_in_l, ctx_tables, tm_ctx)
        yr_c, state = _retention(dl, qc, kc, vc, sgc, zero_state)
        q, k, v, sg, cb, u, gt = _in_projection(x, mod_lat, True, norm_mix_w[l], w_in_l, lat_tables, tm_lat)
        yr, _ = _retention(dl, q, k, v, sg, state)
        z = _short_conv(cb, u, conv_w[l])
        x = _merge_ffn(yr, z, gt, x, mod_lat, True, norm_ffn_w[l], *wts, final_norm_w, last, tm_lat)
        if not last:
            zc = _short_conv(cbc, uc, conv_w[l])
            xc = _merge_ffn(yr_c, zc, gtc, xc, mod_ctx, False, norm_ffn_w[l], *wts, final_norm_w, False, tm_ctx)
    return x
```

```python
import functools

import jax
import jax.numpy as jnp
from jax import lax
from jax.experimental import pallas as pl
from jax.experimental.pallas import tpu as pltpu

F32 = jnp.float32
BF16 = jnp.bfloat16

D_MODEL = 1024
HEADS = 8
DK = 128
DV = 256
QK = HEADS * DK
VW = HEADS * DV
CONV_W = D_MODEL
D_FF = 2816
GRID_W = 64
ROPE_THETA = 10000.0
EPS = 1e-6
OFF_Q, OFF_K, OFF_V, OFF_G = 0, QK, 2 * QK, 2 * QK + VW
OFF_CB = 2 * QK + 2 * VW
OFF_CC = OFF_CB + CONV_W
OFF_CU = OFF_CC + CONV_W
OFF_GT = OFF_CU + CONV_W
IN_COLS = OFF_GT + 2 * D_MODEL

RET_CHUNK = 256
MOD_ROWS = 40
HALO = 16
VMEM_LIMIT = 56 * 1024 * 1024


def _silu(x):
    return x / (1.0 + jnp.exp(-x))


def _sigmoid(x):
    return 1.0 / (1.0 + jnp.exp(-x))


def _log_sigmoid(x):
    return jnp.minimum(x, 0.0) - jnp.log1p(jnp.exp(-jnp.abs(x)))


def _rms_norm(x, w):
    ms = jnp.mean(x * x, axis=-1, keepdims=True)
    return x * lax.rsqrt(ms + EPS) * w


def _dot(a, b):
    return jnp.dot(a, b, preferred_element_type=F32)


def _dot_t(a, b):
    return lax.dot_general(a, b, (((0,), (0,)), ((), ())), preferred_element_type=F32)


def _params(n_axes):
    return pltpu.CompilerParams(dimension_semantics=("arbitrary",) * n_axes,
                                vmem_limit_bytes=VMEM_LIMIT)


def _mod_kernel(c_ref, w_ref, b_ref, o_ref):
    s = _silu(c_ref[...]).astype(BF16)
    o_ref[0] = _dot(s, w_ref[0].astype(BF16)) + b_ref[0]


def _modulation(cvec, ada_w, ada_b):
    depth = ada_w.shape[0]
    nj = ada_w.shape[2] // D_MODEL
    return pl.pallas_call(
        _mod_kernel,
        grid=(depth, nj),
        in_specs=[
            pl.BlockSpec((MOD_ROWS, D_MODEL), lambda l, j: (0, 0)),
            pl.BlockSpec((1, D_MODEL, D_MODEL), lambda l, j: (l, 0, j)),
            pl.BlockSpec((1, 1, D_MODEL), lambda l, j: (l, 0, j)),
        ],
        out_specs=pl.BlockSpec((1, MOD_ROWS, D_MODEL), lambda l, j: (l, 0, j)),
        out_shape=jax.ShapeDtypeStruct((depth, MOD_ROWS, ada_w.shape[2]), F32),
        compiler_params=_params(2),
        name="modulation",
    )(cvec, ada_w, ada_b.reshape(depth, 1, -1))


def _inproj_kernel(x_ref, mod_ref, nw_ref, w_ref, cq_ref, sq_ref, ck_ref, sk_ref,
                   q_ref, k_ref, v_ref, sg_ref, cb_ref, u_ref, gt_ref):
    tm = x_ref.shape[1]
    y = _rms_norm(x_ref[0], nw_ref[...])
    h = (y * (1.0 + mod_ref[0, 1:2, :]) + mod_ref[0, 0:1, :]).astype(BF16)

    def proj(off, width):
        return _dot(h, w_ref[:, off:off + width])

    lane = lax.broadcasted_iota(jnp.int32, (tm, DK), 1)
    first_half = (lane % 64) < 32

    def rope(r, c_ref, s_ref, o_ref):
        c = c_ref[...]
        s = s_ref[...]
        for hd in range(HEADS):
            xh = r[:, hd * DK:(hd + 1) * DK]
            partner = jnp.where(first_half, pltpu.roll(xh, DK - 32, 1), pltpu.roll(xh, 32, 1))
            o_ref[0, :, hd * DK:(hd + 1) * DK] = (xh * c + partner * s).astype(BF16)

    rope(proj(OFF_Q, QK), cq_ref, sq_ref, q_ref)
    rope(proj(OFF_K, QK), ck_ref, sk_ref, k_ref)
    for j in range(VW // D_MODEL):
        cols = slice(j * D_MODEL, (j + 1) * D_MODEL)
        v_ref[0, :, cols] = proj(OFF_V + j * D_MODEL, D_MODEL).astype(BF16)
        sg_ref[0, :, cols] = _silu(proj(OFF_G + j * D_MODEL, D_MODEL)).astype(BF16)
        gt_ref[0, :, cols] = _sigmoid(proj(OFF_GT + j * D_MODEL, D_MODEL)).astype(BF16)
    cb_ref[0] = proj(OFF_CB, CONV_W).astype(BF16)
    u_ref[0] = (proj(OFF_CC, CONV_W) * proj(OFF_CU, CONV_W)).astype(BF16)


def _in_projection(x, mod, per_batch_mod, norm_w, w_in, tables, tm):
    bsz, t, _ = x.shape
    mod_map = (lambda b, i: (b, 0, 0)) if per_batch_mod else (lambda b, i: (0, 0, 0))
    tab_spec = pl.BlockSpec((tm, DK), lambda b, i: (i, 0))
    tok = lambda w: pl.BlockSpec((1, tm, w), lambda b, i: (b, i, 0))
    widths = (QK, QK, VW, VW, CONV_W, CONV_W, 2 * D_MODEL)
    return pl.pallas_call(
        _inproj_kernel,
        grid=(bsz, t // tm),
        in_specs=[
            tok(D_MODEL),
            pl.BlockSpec((1, 6, D_MODEL), mod_map),
            pl.BlockSpec((1, D_MODEL), lambda b, i: (0, 0)),
            pl.BlockSpec((D_MODEL, IN_COLS), lambda b, i: (0, 0)),
            tab_spec, tab_spec, tab_spec, tab_spec,
        ],
        out_specs=[tok(w) for w in widths],
        out_shape=[jax.ShapeDtypeStruct((bsz, t, w), BF16) for w in widths],
        compiler_params=_params(2),
        name="in_projection",
    )(x, mod, norm_w.reshape(1, D_MODEL), w_in, *tables)


def _key_decays(lgf, lgb):
    ri = lax.broadcasted_iota(jnp.int32, (RET_CHUNK, DK), 0).astype(F32)
    return jnp.exp((RET_CHUNK - 1.0 - ri) * lgf), jnp.exp(ri * lgb)


def _ret_kernel(*refs, n_chunks, has_init, emit_state):
    refs = list(refs)
    dl_ref, q_ref, k_ref, v_ref, sg_ref = refs[:5]
    pos = 5
    s0_ref = None
    if has_init:
        s0_ref = refs[pos]
        pos += 1
    yr_ref = refs[pos]
    pos += 1
    sfin_ref = None
    if emit_state:
        sfin_ref = refs[pos]
        pos += 1
    mask_ref, dec_ref, cd_ref, u_ref, ss_ref = refs[pos:]
    c_len = RET_CHUNK

    @pl.when(pl.program_id(1) == 0)
    def _():
        lgf = _log_sigmoid(dl_ref[0, 0, 0:1, :])
        lgb = _log_sigmoid(dl_ref[1, 0, 0:1, :])
        ii = lax.broadcasted_iota(jnp.int32, (c_len, c_len), 0).astype(F32)
        jj = lax.broadcasted_iota(jnp.int32, (c_len, c_len), 1).astype(F32)
        diff = ii - jj
        mask_ref[...] = (jnp.where(diff >= 0, jnp.exp(jnp.maximum(diff, 0.0) * lgf), 0.0)
                         + jnp.where(diff <= 0, jnp.exp(jnp.maximum(-diff, 0.0) * lgb), 0.0))
        ri = lax.broadcasted_iota(jnp.int32, (c_len, DK), 0).astype(F32)
        dec_ref[0] = jnp.exp((ri + 1.0) * lgf[:, :DK])
        dec_ref[1] = jnp.exp((c_len - ri) * lgb[:, :DK])
        k_dec_f, k_dec_b = _key_decays(lgf[:, :DK], lgb[:, :DK])
        dec_ref[2] = k_dec_f
        dec_ref[3] = k_dec_b
        cd_ref[0] = jnp.broadcast_to(jnp.exp(c_len * lgf), (8, DV))
        cd_ref[1] = jnp.broadcast_to(jnp.exp(c_len * lgb), (8, DV))

    chunk_dec_f = cd_ref[0, 0:1, :]
    chunk_dec_b = cd_ref[1, 0:1, :]
    rows = [slice(c * c_len, (c + 1) * c_len) for c in range(n_chunks)]

    for c in range(n_chunks):
        kc = k_ref[0, rows[c], :].astype(F32)
        kk = jnp.concatenate([kc * dec_ref[2], kc * dec_ref[3]], axis=1).astype(BF16)
        u_ref[c] = _dot_t(kk, v_ref[0, rows[c], :])

    s = s0_ref[0, 0, 0:DK, :] if has_init else jnp.zeros((DK, DV), F32)
    for c in range(n_chunks):
        ss_ref[c, 0:DK, :] = s.astype(BF16)
        s = s * chunk_dec_f + u_ref[c, 0:DK, :]
    if emit_state:
        sfin_ref[0, 0, 0:DK, :] = s
    s = s0_ref[0, 0, DK:2 * DK, :] if has_init else jnp.zeros((DK, DV), F32)
    for c in reversed(range(n_chunks)):
        ss_ref[c, DK:2 * DK, :] = s.astype(BF16)
        s = s * chunk_dec_b + u_ref[c, DK:2 * DK, :]
    if emit_state:
        sfin_ref[0, 0, DK:2 * DK, :] = s

    for c in range(n_chunks):
        qc = q_ref[0, rows[c], :]
        scores = lax.dot_general(qc, k_ref[0, rows[c], :], (((1,), (1,)), ((), ())),
                                 preferred_element_type=F32)
        p = (scores * mask_ref[...]).astype(BF16)
        qf = qc.astype(F32)
        qq = jnp.concatenate([qf * dec_ref[0], qf * dec_ref[1]], axis=1).astype(BF16)
        o = _dot(p, v_ref[0, rows[c], :]) + _dot(qq, ss_ref[c])
        mu = jnp.mean(o, axis=-1, keepdims=True)
        oc = o - mu
        var = jnp.mean(oc * oc, axis=-1, keepdims=True)
        on = oc * lax.rsqrt(var + EPS)
        yr_ref[0, rows[c], :] = (on * sg_ref[0, rows[c], :].astype(F32)).astype(BF16)


def _retention(dl, q, k, v, sg, s0, emit_state):
    bsz, t, _ = q.shape
    n_chunks = t // RET_CHUNK
    has_init = s0 is not None
    tok = lambda w: pl.BlockSpec((1, t, w), lambda hd, b: (b, 0, hd))
    st = pl.BlockSpec((1, 1, 2 * DK, DV), lambda hd, b: (b, hd, 0, 0))
    out_specs = [tok(DV)]
    out_shape = [jax.ShapeDtypeStruct((bsz, t, VW), BF16)]
    if emit_state:
        out_specs.append(st)
        out_shape.append(jax.ShapeDtypeStruct((bsz, HEADS, 2 * DK, DV), F32))
    args = (dl, q, k, v, sg) + ((s0,) if has_init else ())
    return pl.pallas_call(
        functools.partial(_ret_kernel, n_chunks=n_chunks, has_init=has_init, emit_state=emit_state),
        grid=(HEADS, bsz),
        in_specs=[pl.BlockSpec((2, 1, 8, DV), lambda hd, b: (0, hd, 0, 0)),
                  tok(DK), tok(DK), tok(DV), tok(DV)] + ([st] if has_init else []),
        out_specs=out_specs,
        out_shape=out_shape,
        scratch_shapes=[pltpu.VMEM((RET_CHUNK, RET_CHUNK), F32),
                        pltpu.VMEM((4, RET_CHUNK, DK), F32),
                        pltpu.VMEM((2, 8, DV), F32),
                        pltpu.VMEM((n_chunks, 2 * DK, DV), F32),
                        pltpu.VMEM((n_chunks, 2 * DK, DV), BF16)],
        compiler_params=_params(2),
        name="retention",
    )(*args)


def _ctx_state_kernel(x_ref, mod_ref, nw_ref, w_ref, dl_ref, st_ref, *, n_chunks, k_scale):
    c_len = RET_CHUNK
    y = _rms_norm(x_ref[0], nw_ref[...])
    h = (y * (1.0 + mod_ref[0, 1:2, :]) + mod_ref[0, 0:1, :]).astype(BF16)
    incs = []
    for c in range(n_chunks):
        hc = h[c * c_len:(c + 1) * c_len, :]
        incs.append((_dot(hc, w_ref[:, 0:QK]) * k_scale, _dot(hc, w_ref[:, QK:QK + VW]).astype(BF16)))
    for hd in range(HEADS):
        lgf = _log_sigmoid(dl_ref[0, hd, 0:1, :])
        lgb = _log_sigmoid(dl_ref[1, hd, 0:1, :])
        k_dec_f, k_dec_b = _key_decays(lgf[:, :DK], lgb[:, :DK])
        chunk_dec_f = jnp.exp(c_len * lgf)
        chunk_dec_b = jnp.exp(c_len * lgb)
        u = []
        for kf, vb in incs:
            kh = kf[:, hd * DK:(hd + 1) * DK]
            kk = jnp.concatenate([kh * k_dec_f, kh * k_dec_b], axis=1).astype(BF16)
            u.append(_dot_t(kk, vb[:, hd * DV:(hd + 1) * DV]))
        s_f = u[0][0:DK, :]
        for c in range(1, n_chunks):
            s_f = s_f * chunk_dec_f + u[c][0:DK, :]
        s_b = u[n_chunks - 1][DK:2 * DK, :]
        for c in reversed(range(n_chunks - 1)):
            s_b = s_b * chunk_dec_b + u[c][DK:2 * DK, :]
        st_ref[0, hd, 0:DK, :] = s_f
        st_ref[0, hd, DK:2 * DK, :] = s_b


def _context_state(xc, mod, norm_w, w_kv, dl, k_scale):
    bsz, t, _ = xc.shape
    return pl.pallas_call(
        functools.partial(_ctx_state_kernel, n_chunks=t // RET_CHUNK, k_scale=k_scale),
        grid=(bsz,),
        in_specs=[
            pl.BlockSpec((1, t, D_MODEL), lambda b: (b, 0, 0)),
            pl.BlockSpec((1, 6, D_MODEL), lambda b: (0, 0, 0)),
            pl.BlockSpec((1, D_MODEL), lambda b: (0, 0)),
            pl.BlockSpec((D_MODEL, QK + VW), lambda b: (0, 0)),
            pl.BlockSpec((2, HEADS, 8, DV), lambda b: (0, 0, 0, 0)),
        ],
        out_specs=pl.BlockSpec((1, HEADS, 2 * DK, DV), lambda b: (b, 0, 0, 0)),
        out_shape=jax.ShapeDtypeStruct((bsz, HEADS, 2 * DK, DV), F32),
        compiler_params=_params(1),
        name="context_state",
    )(xc, mod, norm_w.reshape(1, D_MODEL), w_kv, dl)


FF_CHUNKS = ((0, 1024), (1024, 2048), (2048, D_FF))


def _merge_ffn_kernel(yr_ref, cb_ref, u_ref, up_ref, un_ref, gt_ref, x_ref, mod_ref, nfw_ref, cw_ref,
                      wro_ref, wco_ref, wo_ref, wfi_ref, wfo_ref, fnw_ref, o_ref, *, final_norm):
    tm = x_ref.shape[1]
    i = pl.program_id(1)
    u = u_ref[0].astype(F32)
    has_prev = jnp.where(i > 0, 1.0, 0.0)
    has_next = jnp.where(i < pl.num_programs(1) - 1, 1.0, 0.0)
    head_row = up_ref[0].astype(F32)[HALO - 1:HALO, :] * has_prev
    tail_row = un_ref[0].astype(F32)[0:1, :] * has_next
    pos = lax.broadcasted_iota(jnp.int32, u.shape, 0)
    prev = jnp.where(pos == 0, head_row, pltpu.roll(u, 1, 0))
    nxt = jnp.where(pos == tm - 1, tail_row, pltpu.roll(u, tm - 1, 0))
    conv = prev * cw_ref[0:1, :] + u * cw_ref[1:2, :] + nxt * cw_ref[2:3, :]
    z = (cb_ref[0].astype(F32) * conv).astype(BF16)

    y_ret = _dot(yr_ref[0], wro_ref[...])
    y_conv = _dot(z, wco_ref[...])
    mix = (gt_ref[0, :, 0:D_MODEL].astype(F32) * y_ret
           + gt_ref[0, :, D_MODEL:2 * D_MODEL].astype(F32) * y_conv).astype(BF16)
    x1 = x_ref[0] + mod_ref[0, 2:3, :] * _dot(mix, wo_ref[...])
    h = (_rms_norm(x1, nfw_ref[...]) * (1.0 + mod_ref[0, 4:5, :]) + mod_ref[0, 3:4, :]).astype(BF16)
    acc = None
    for c0, c1 in FF_CHUNKS:
        a = (_silu(_dot(h, wfi_ref[:, c0:c1])) * _dot(h, wfi_ref[:, D_FF + c0:D_FF + c1])).astype(BF16)
        part = _dot(a, wfo_ref[c0:c1, :])
        acc = part if acc is None else acc + part
    x2 = x1 + mod_ref[0, 5:6, :] * acc
    if final_norm:
        x2 = _rms_norm(x2, fnw_ref[...])
    o_ref[0] = x2


def _merge_ffn(yr, cb, u, gt, x, mod, per_batch_mod, norm_ffn_w, conv_w, w_ret_out, w_conv_out, w_o,
               w_ffn_in, w_ffn_out, final_norm_w, final_norm, tm):
    bsz, t, _ = x.shape
    mod_map = (lambda b, i: (b, 0, 0)) if per_batch_mod else (lambda b, i: (0, 0, 0))
    tok = lambda w: pl.BlockSpec((1, tm, w), lambda b, i: (b, i, 0))
    full = lambda a: pl.BlockSpec(a.shape, lambda b, i: (0,) * a.ndim)
    per_tile = tm // HALO
    last_halo = t // HALO - 1
    halo_prev = pl.BlockSpec((1, HALO, CONV_W), lambda b, i: (b, jnp.maximum(i * per_tile - 1, 0), 0))
    halo_next = pl.BlockSpec((1, HALO, CONV_W), lambda b, i: (b, jnp.minimum((i + 1) * per_tile, last_halo), 0))
    nfw = norm_ffn_w.reshape(1, D_MODEL)
    fnw = final_norm_w.reshape(1, D_MODEL)
    weights = (w_ret_out, w_conv_out, w_o, w_ffn_in, w_ffn_out)
    return pl.pallas_call(
        functools.partial(_merge_ffn_kernel, final_norm=final_norm),
        grid=(bsz, t // tm),
        in_specs=[tok(VW), tok(CONV_W), tok(CONV_W), halo_prev, halo_next, tok(2 * D_MODEL), tok(D_MODEL),
                  pl.BlockSpec((1, 6, D_MODEL), mod_map), full(nfw), full(conv_w),
                  *[full(w) for w in weights], full(fnw)],
        out_specs=tok(D_MODEL),
        out_shape=jax.ShapeDtypeStruct((bsz, t, D_MODEL), F32),
        compiler_params=_params(2),
        name="merge_ffn",
    )(yr, cb, u, u, u, gt, x, mod, nfw, conv_w, *weights, fnw)


def _rope_tables(t, k_scale):
    pos = jnp.arange(t)
    rows = (pos // GRID_W).astype(F32)
    cols = (pos % GRID_W).astype(F32)
    freqs = ROPE_THETA ** (-jnp.arange(0, 64, 2, dtype=F32) / 64)
    ang = jnp.concatenate([rows[:, None] * freqs[None, :]] * 2 + [cols[:, None] * freqs[None, :]] * 2, axis=1)
    sign = jnp.tile(jnp.concatenate([-jnp.ones((32,), F32), jnp.ones((32,), F32)]), 2)
    cos, sin = jnp.cos(ang), jnp.sin(ang) * sign[None, :]
    return cos, sin, cos * k_scale, sin * k_scale


def _plain_tables(t, k_scale):
    one = jnp.ones((t, DK), F32)
    zero = jnp.zeros((t, DK), F32)
    return one, zero, one * k_scale, zero


def kernel(x, c, ctx, c_ctx, norm_mix_w, ada_w, ada_b, w_in, ret_decay_logit, conv_w, w_ret_out,
           w_conv_out, w_o, norm_ffn_w, w_ffn_in, w_ffn_out, final_norm_w):
    bsz, seq, _ = x.shape
    t_ctx = ctx.shape[1]
    depth = w_in.shape[0]
    assert seq % RET_CHUNK == 0 and t_ctx % RET_CHUNK == 0 and bsz < MOD_ROWS
    k_scale = DK ** -0.5
    tm_lat = 256
    tm_ctx = 256

    cvec = jnp.zeros((MOD_ROWS, D_MODEL), F32).at[:bsz].set(c).at[bsz].set(c_ctx)
    mod = _modulation(cvec, ada_w, ada_b).reshape(depth, MOD_ROWS, 6, D_MODEL)
    lat_tables = _rope_tables(seq, k_scale)
    ctx_tables = _plain_tables(t_ctx, k_scale)

    xc = ctx
    for l in range(depth):
        last = l == depth - 1
        w_in_l = w_in[l].astype(BF16)
        wts = (w_ret_out[l].astype(BF16), w_conv_out[l].astype(BF16), w_o[l].astype(BF16),
               w_ffn_in[l].astype(BF16), w_ffn_out[l].astype(BF16))
        dl = jnp.broadcast_to(ret_decay_logit[l][:, :, None, None], (2, HEADS, 8, DV))
        mod_lat = mod[l, :bsz]
        mod_ctx = mod[l, bsz:bsz + 1]

        if last:
            state = _context_state(xc, mod_ctx, norm_mix_w[l], w_in_l[:, OFF_K:OFF_G], dl, k_scale)
        else:
            qc, kc, vc, sgc, cbc, uc, gtc = _in_projection(xc, mod_ctx, False, norm_mix_w[l], w_in_l,
                                                           ctx_tables, tm_ctx)
            yr_c, state = _retention(dl, qc, kc, vc, sgc, None, True)
        q, k, v, sg, cb, u, gt = _in_projection(x, mod_lat, True, norm_mix_w[l], w_in_l, lat_tables, tm_lat)
        yr = _retention(dl, q, k, v, sg, state, False)[0]
        x = _merge_ffn(yr, cb, u, gt, x, mod_lat, True, norm_ffn_w[l], conv_w[l], *wts, final_norm_w,
                       last, tm_lat)
        if not last:
            xc = _merge_ffn(yr_c, cbc, uc, gtc, xc, mod_ctx, False, norm_ffn_w[l], conv_w[l], *wts,
                            final_norm_w, False, tm_ctx)
    return x
```

```python
import functools

import jax
import jax.numpy as jnp
from jax import lax
from jax.experimental import pallas as pl
from jax.experimental.pallas import tpu as pltpu

F32 = jnp.float32
BF16 = jnp.bfloat16

D_MODEL = 1024
HEADS = 8
DK = 128
DV = 256
QK = HEADS * DK
VW = HEADS * DV
CONV_W = D_MODEL
D_FF = 2816
GRID_W = 64
ROPE_THETA = 10000.0
EPS = 1e-6
OFF_Q, OFF_K, OFF_V, OFF_G = 0, QK, 2 * QK, 2 * QK + VW
OFF_CB = 2 * QK + 2 * VW
OFF_CC = OFF_CB + CONV_W
OFF_CU = OFF_CC + CONV_W
OFF_GT = OFF_CU + CONV_W
IN_COLS = OFF_GT + 2 * D_MODEL

RET_CHUNK = 256
MOD_ROWS = 40
SUB_ROWS = 256
HALO = 16
VMEM_LIMIT = 56 * 1024 * 1024


def _silu(x):
    return x / (1.0 + jnp.exp(-x))


def _sigmoid(x):
    return 1.0 / (1.0 + jnp.exp(-x))


def _log_sigmoid(x):
    return jnp.minimum(x, 0.0) - jnp.log1p(jnp.exp(-jnp.abs(x)))


def _rms_norm(x, w):
    ms = jnp.mean(x * x, axis=-1, keepdims=True)
    return x * lax.rsqrt(ms + EPS) * w


def _dot(a, b):
    return jnp.dot(a, b, preferred_element_type=F32)


def _dot_t(a, b):
    return lax.dot_general(a, b, (((0,), (0,)), ((), ())), preferred_element_type=F32)


def _params(n_axes):
    return pltpu.CompilerParams(dimension_semantics=("arbitrary",) * n_axes,
                                vmem_limit_bytes=VMEM_LIMIT)


def _mod_kernel(c_ref, w_ref, b_ref, o_ref):
    s = _silu(c_ref[...]).astype(BF16)
    o_ref[0] = _dot(s, w_ref[0].astype(BF16)) + b_ref[0]


def _modulation(cvec, ada_w, ada_b):
    depth = ada_w.shape[0]
    nj = ada_w.shape[2] // D_MODEL
    return pl.pallas_call(
        _mod_kernel,
        grid=(depth, nj),
        in_specs=[
            pl.BlockSpec((MOD_ROWS, D_MODEL), lambda l, j: (0, 0)),
            pl.BlockSpec((1, D_MODEL, D_MODEL), lambda l, j: (l, 0, j)),
            pl.BlockSpec((1, 1, D_MODEL), lambda l, j: (l, 0, j)),
        ],
        out_specs=pl.BlockSpec((1, MOD_ROWS, D_MODEL), lambda l, j: (l, 0, j)),
        out_shape=jax.ShapeDtypeStruct((depth, MOD_ROWS, ada_w.shape[2]), F32),
        compiler_params=_params(2),
        name="modulation",
    )(cvec, ada_w, ada_b.reshape(depth, 1, -1))


def _inproj_kernel(x_ref, mod_ref, nw_ref, w_ref, cq_ref, sq_ref, ck_ref, sk_ref,
                   q_ref, k_ref, v_ref, sg_ref, cb_ref, u_ref, gt_ref):
    tm = x_ref.shape[1]
    sub = min(tm, SUB_ROWS)
    lane = lax.broadcasted_iota(jnp.int32, (sub, DK), 1)
    first_half = (lane % 64) < 32

    subs = [slice(r0, r0 + sub) for r0 in range(0, tm, sub)]
    hs = []
    for rs in subs:
        y = _rms_norm(x_ref[0, rs, :], nw_ref[...])
        hs.append((y * (1.0 + mod_ref[0, 1:2, :]) + mod_ref[0, 0:1, :]).astype(BF16))

    def proj(h, off, width):
        return _dot(h, w_ref[:, off:off + width])

    def rope(r, c_ref, s_ref, o_ref, rs):
        c = c_ref[rs, :]
        s = s_ref[rs, :]
        for hd in range(HEADS):
            xh = r[:, hd * DK:(hd + 1) * DK]
            partner = jnp.where(first_half, pltpu.roll(xh, DK - 32, 1), pltpu.roll(xh, 32, 1))
            o_ref[0, rs, hd * DK:(hd + 1) * DK] = (xh * c + partner * s).astype(BF16)

    for h, rs in zip(hs, subs):
        rope(proj(h, OFF_Q, QK), cq_ref, sq_ref, q_ref, rs)
    for h, rs in zip(hs, subs):
        rope(proj(h, OFF_K, QK), ck_ref, sk_ref, k_ref, rs)
    for j in range(VW // D_MODEL):
        cols = slice(j * D_MODEL, (j + 1) * D_MODEL)
        for h, rs in zip(hs, subs):
            v_ref[0, rs, cols] = proj(h, OFF_V + j * D_MODEL, D_MODEL).astype(BF16)
        for h, rs in zip(hs, subs):
            sg_ref[0, rs, cols] = _silu(proj(h, OFF_G + j * D_MODEL, D_MODEL)).astype(BF16)
        for h, rs in zip(hs, subs):
            gt_ref[0, rs, cols] = _sigmoid(proj(h, OFF_GT + j * D_MODEL, D_MODEL)).astype(BF16)
    for h, rs in zip(hs, subs):
        cb_ref[0, rs, :] = proj(h, OFF_CB, CONV_W).astype(BF16)
    for h, rs in zip(hs, subs):
        u_ref[0, rs, :] = (proj(h, OFF_CC, CONV_W) * proj(h, OFF_CU, CONV_W)).astype(BF16)


def _in_projection(x, mod, per_batch_mod, norm_w, w_in, tables, tm):
    bsz, t, _ = x.shape
    mod_map = (lambda b, i: (b, 0, 0)) if per_batch_mod else (lambda b, i: (0, 0, 0))
    tab_spec = pl.BlockSpec((tm, DK), lambda b, i: (i, 0))
    tok = lambda w: pl.BlockSpec((1, tm, w), lambda b, i: (b, i, 0))
    widths = (QK, QK, VW, VW, CONV_W, CONV_W, 2 * D_MODEL)
    return pl.pallas_call(
        _inproj_kernel,
        grid=(bsz, t // tm),
        in_specs=[
            tok(D_MODEL),
            pl.BlockSpec((1, 6, D_MODEL), mod_map),
            pl.BlockSpec((1, D_MODEL), lambda b, i: (0, 0)),
            pl.BlockSpec((D_MODEL, IN_COLS), lambda b, i: (0, 0)),
            tab_spec, tab_spec, tab_spec, tab_spec,
        ],
        out_specs=[tok(w) for w in widths],
        out_shape=[jax.ShapeDtypeStruct((bsz, t, w), BF16) for w in widths],
        compiler_params=_params(2),
        name="in_projection",
    )(x, mod, norm_w.reshape(1, D_MODEL), w_in, *tables)


def _key_decays(lgf, lgb):
    ri = lax.broadcasted_iota(jnp.int32, (RET_CHUNK, DK), 0).astype(F32)
    return jnp.exp((RET_CHUNK - 1.0 - ri) * lgf), jnp.exp(ri * lgb)


def _ret_kernel(*refs, n_chunks, n_heads, has_init, emit_state):
    refs = list(refs)
    dl_ref, q_ref, k_ref, v_ref, sg_ref = refs[:5]
    pos = 5
    s0_ref = None
    if has_init:
        s0_ref = refs[pos]
        pos += 1
    yr_ref = refs[pos]
    pos += 1
    sfin_ref = None
    if emit_state:
        sfin_ref = refs[pos]
        pos += 1
    mask_ref, dec_ref, cd_ref, u_ref, ss_ref = refs[pos:]
    c_len = RET_CHUNK

    @pl.when(pl.program_id(1) == 0)
    def _():
        ii = lax.broadcasted_iota(jnp.int32, (c_len, c_len), 0).astype(F32)
        jj = lax.broadcasted_iota(jnp.int32, (c_len, c_len), 1).astype(F32)
        diff = ii - jj
        ri = lax.broadcasted_iota(jnp.int32, (c_len, DK), 0).astype(F32)
        for hd in range(n_heads):
            lgf = _log_sigmoid(dl_ref[0, hd, 0:1, :])
            lgb = _log_sigmoid(dl_ref[1, hd, 0:1, :])
            mask_ref[hd] = (jnp.where(diff >= 0, jnp.exp(jnp.maximum(diff, 0.0) * lgf), 0.0)
                            + jnp.where(diff <= 0, jnp.exp(jnp.maximum(-diff, 0.0) * lgb), 0.0))
            dec_ref[hd, 0] = jnp.exp((ri + 1.0) * lgf[:, :DK])
            dec_ref[hd, 1] = jnp.exp((c_len - ri) * lgb[:, :DK])
            k_dec_f, k_dec_b = _key_decays(lgf[:, :DK], lgb[:, :DK])
            dec_ref[hd, 2] = k_dec_f
            dec_ref[hd, 3] = k_dec_b
            cd_ref[hd, 0] = jnp.broadcast_to(jnp.exp(c_len * lgf), (8, DV))
            cd_ref[hd, 1] = jnp.broadcast_to(jnp.exp(c_len * lgb), (8, DV))

    rows = [slice(c * c_len, (c + 1) * c_len) for c in range(n_chunks)]
    for hd in range(n_heads):
        kcols = slice(hd * DK, (hd + 1) * DK)
        vcols = slice(hd * DV, (hd + 1) * DV)
        chunk_dec_f = cd_ref[hd, 0, 0:1, :]
        chunk_dec_b = cd_ref[hd, 1, 0:1, :]

        for c in range(n_chunks):
            kc = k_ref[0, rows[c], kcols].astype(F32)
            kk = jnp.concatenate([kc * dec_ref[hd, 2], kc * dec_ref[hd, 3]], axis=1).astype(BF16)
            u_ref[hd, c] = _dot_t(kk, v_ref[0, rows[c], vcols])

        s = s0_ref[0, hd, 0:DK, :] if has_init else jnp.zeros((DK, DV), F32)
        for c in range(n_chunks):
            ss_ref[hd, c, 0:DK, :] = s.astype(BF16)
            s = s * chunk_dec_f + u_ref[hd, c, 0:DK, :]
        if emit_state:
            sfin_ref[0, hd, 0:DK, :] = s
        s = s0_ref[0, hd, DK:2 * DK, :] if has_init else jnp.zeros((DK, DV), F32)
        for c in reversed(range(n_chunks)):
            ss_ref[hd, c, DK:2 * DK, :] = s.astype(BF16)
            s = s * chunk_dec_b + u_ref[hd, c, DK:2 * DK, :]
        if emit_state:
            sfin_ref[0, hd, DK:2 * DK, :] = s

        for c in range(n_chunks):
            qc = q_ref[0, rows[c], kcols]
            scores = lax.dot_general(qc, k_ref[0, rows[c], kcols], (((1,), (1,)), ((), ())),
                                     preferred_element_type=F32)
            p = (scores * mask_ref[hd]).astype(BF16)
            qf = qc.astype(F32)
            qq = jnp.concatenate([qf * dec_ref[hd, 0], qf * dec_ref[hd, 1]], axis=1).astype(BF16)
            o = _dot(p, v_ref[0, rows[c], vcols]) + _dot(qq, ss_ref[hd, c])
            mu = jnp.mean(o, axis=-1, keepdims=True)
            oc = o - mu
            var = jnp.mean(oc * oc, axis=-1, keepdims=True)
            on = oc * lax.rsqrt(var + EPS)
            yr_ref[0, rows[c], vcols] = on.astype(BF16) * sg_ref[0, rows[c], vcols]


def _retention(dl, q, k, v, sg, s0, emit_state, n_heads):
    bsz, t, _ = q.shape
    n_chunks = t // RET_CHUNK
    has_init = s0 is not None
    tok = lambda w: pl.BlockSpec((1, t, n_heads * w), lambda g, b: (b, 0, g))
    st = pl.BlockSpec((1, n_heads, 2 * DK, DV), lambda g, b: (b, g, 0, 0))
    out_specs = [tok(DV)]
    out_shape = [jax.ShapeDtypeStruct((bsz, t, VW), BF16)]
    if emit_state:
        out_specs.append(st)
        out_shape.append(jax.ShapeDtypeStruct((bsz, HEADS, 2 * DK, DV), F32))
    args = (dl, q, k, v, sg) + ((s0,) if has_init else ())
    return pl.pallas_call(
        functools.partial(_ret_kernel, n_chunks=n_chunks, n_heads=n_heads, has_init=has_init,
                          emit_state=emit_state),
        grid=(HEADS // n_heads, bsz),
        in_specs=[pl.BlockSpec((2, n_heads, 8, DV), lambda g, b: (0, g, 0, 0)),
                  tok(DK), tok(DK), tok(DV), tok(DV)] + ([st] if has_init else []),
        out_specs=out_specs,
        out_shape=out_shape,
        scratch_shapes=[pltpu.VMEM((n_heads, RET_CHUNK, RET_CHUNK), F32),
                        pltpu.VMEM((n_heads, 4, RET_CHUNK, DK), F32),
                        pltpu.VMEM((n_heads, 2, 8, DV), F32),
                        pltpu.VMEM((n_heads, n_chunks, 2 * DK, DV), F32),
                        pltpu.VMEM((n_heads, n_chunks, 2 * DK, DV), BF16)],
        compiler_params=_params(2),
        name="retention",
    )(*args)


def _ctx_state_kernel(x_ref, mod_ref, nw_ref, w_ref, dl_ref, st_ref, *, n_chunks, k_scale):
    c_len = RET_CHUNK
    y = _rms_norm(x_ref[0], nw_ref[...])
    h = (y * (1.0 + mod_ref[0, 1:2, :]) + mod_ref[0, 0:1, :]).astype(BF16)
    incs = []
    for c in range(n_chunks):
        hc = h[c * c_len:(c + 1) * c_len, :]
        incs.append((_dot(hc, w_ref[:, 0:QK]) * k_scale, _dot(hc, w_ref[:, QK:QK + VW]).astype(BF16)))
    for hd in range(HEADS):
        lgf = _log_sigmoid(dl_ref[0, hd, 0:1, :])
        lgb = _log_sigmoid(dl_ref[1, hd, 0:1, :])
        k_dec_f, k_dec_b = _key_decays(lgf[:, :DK], lgb[:, :DK])
        chunk_dec_f = jnp.exp(c_len * lgf)
        chunk_dec_b = jnp.exp(c_len * lgb)
        u = []
        for kf, vb in incs:
            kh = kf[:, hd * DK:(hd + 1) * DK]
            kk = jnp.concatenate([kh * k_dec_f, kh * k_dec_b], axis=1).astype(BF16)
            u.append(_dot_t(kk, vb[:, hd * DV:(hd + 1) * DV]))
        s_f = u[0][0:DK, :]
        for c in range(1, n_chunks):
            s_f = s_f * chunk_dec_f + u[c][0:DK, :]
        s_b = u[n_chunks - 1][DK:2 * DK, :]
        for c in reversed(range(n_chunks - 1)):
            s_b = s_b * chunk_dec_b + u[c][DK:2 * DK, :]
        st_ref[0, hd, 0:DK, :] = s_f
        st_ref[0, hd, DK:2 * DK, :] = s_b


def _context_state(xc, mod, norm_w, w_kv, dl, k_scale):
    bsz, t, _ = xc.shape
    return pl.pallas_call(
        functools.partial(_ctx_state_kernel, n_chunks=t // RET_CHUNK, k_scale=k_scale),
        grid=(bsz,),
        in_specs=[
            pl.BlockSpec((1, t, D_MODEL), lambda b: (b, 0, 0)),
            pl.BlockSpec((1, 6, D_MODEL), lambda b: (0, 0, 0)),
            pl.BlockSpec((1, D_MODEL), lambda b: (0, 0)),
            pl.BlockSpec((D_MODEL, QK + VW), lambda b: (0, 0)),
            pl.BlockSpec((2, HEADS, 8, DV), lambda b: (0, 0, 0, 0)),
        ],
        out_specs=pl.BlockSpec((1, HEADS, 2 * DK, DV), lambda b: (b, 0, 0, 0)),
        out_shape=jax.ShapeDtypeStruct((bsz, HEADS, 2 * DK, DV), F32),
        compiler_params=_params(1),
        name="context_state",
    )(xc, mod, norm_w.reshape(1, D_MODEL), w_kv, dl)


FF_CHUNKS = ((0, 1024), (1024, 2048), (2048, D_FF))


def _merge_ffn_kernel(yr_ref, cb_ref, u_ref, up_ref, un_ref, gt_ref, x_ref, mod_ref, nfw_ref, cw_ref,
                      wro_ref, wco_ref, wo_ref, wfi_ref, wfo_ref, fnw_ref, o_ref, *, final_norm):
    tm = x_ref.shape[1]
    i = pl.program_id(1)
    u = u_ref[0].astype(F32)
    has_prev = jnp.where(i > 0, 1.0, 0.0)
    has_next = jnp.where(i < pl.num_programs(1) - 1, 1.0, 0.0)
    head_row = up_ref[0].astype(F32)[HALO - 1:HALO, :] * has_prev
    tail_row = un_ref[0].astype(F32)[0:1, :] * has_next
    pos = lax.broadcasted_iota(jnp.int32, u.shape, 0)
    prev = jnp.where(pos == 0, head_row, pltpu.roll(u, 1, 0))
    nxt = jnp.where(pos == tm - 1, tail_row, pltpu.roll(u, tm - 1, 0))
    conv = prev * cw_ref[0:1, :] + u * cw_ref[1:2, :] + nxt * cw_ref[2:3, :]
    z = (cb_ref[0].astype(F32) * conv).astype(BF16)

    sub = min(tm, SUB_ROWS)
    subs = [slice(r0, r0 + sub) for r0 in range(0, tm, sub)]
    y_ret = [_dot(yr_ref[0, rs, :], wro_ref[...]) for rs in subs]
    y_conv = [_dot(z[rs, :], wco_ref[...]) for rs in subs]
    mix = [(gt_ref[0, rs, 0:D_MODEL].astype(F32) * yr + gt_ref[0, rs, D_MODEL:2 * D_MODEL].astype(F32) * yc
            ).astype(BF16) for rs, yr, yc in zip(subs, y_ret, y_conv)]
    x1 = [x_ref[0, rs, :] + mod_ref[0, 2:3, :] * _dot(m, wo_ref[...]) for rs, m in zip(subs, mix)]
    hs = [(_rms_norm(xx, nfw_ref[...]) * (1.0 + mod_ref[0, 4:5, :]) + mod_ref[0, 3:4, :]).astype(BF16)
          for xx in x1]
    acc = [None] * len(subs)
    for c0, c1 in FF_CHUNKS:
        for n, h in enumerate(hs):
            a = (_silu(_dot(h, wfi_ref[:, c0:c1])) * _dot(h, wfi_ref[:, D_FF + c0:D_FF + c1])).astype(BF16)
            part = _dot(a, wfo_ref[c0:c1, :])
            acc[n] = part if acc[n] is None else acc[n] + part
    for rs, xx, ff in zip(subs, x1, acc):
        x2 = xx + mod_ref[0, 5:6, :] * ff
        if final_norm:
            x2 = _rms_norm(x2, fnw_ref[...])
        o_ref[0, rs, :] = x2


def _merge_ffn(yr, cb, u, gt, x, mod, per_batch_mod, norm_ffn_w, conv_w, w_ret_out, w_conv_out, w_o,
               w_ffn_in, w_ffn_out, final_norm_w, final_norm, tm):
    bsz, t, _ = x.shape
    mod_map = (lambda b, i: (b, 0, 0)) if per_batch_mod else (lambda b, i: (0, 0, 0))
    tok = lambda w: pl.BlockSpec((1, tm, w), lambda b, i: (b, i, 0))
    full = lambda a: pl.BlockSpec(a.shape, lambda b, i: (0,) * a.ndim)
    per_tile = tm // HALO
    last_halo = t // HALO - 1
    halo_prev = pl.BlockSpec((1, HALO, CONV_W), lambda b, i: (b, jnp.maximum(i * per_tile - 1, 0), 0))
    halo_next = pl.BlockSpec((1, HALO, CONV_W), lambda b, i: (b, jnp.minimum((i + 1) * per_tile, last_halo), 0))
    nfw = norm_ffn_w.reshape(1, D_MODEL)
    fnw = final_norm_w.reshape(1, D_MODEL)
    weights = (w_ret_out, w_conv_out, w_o, w_ffn_in, w_ffn_out)
    return pl.pallas_call(
        functools.partial(_merge_ffn_kernel, final_norm=final_norm),
        grid=(bsz, t // tm),
        in_specs=[tok(VW), tok(CONV_W), tok(CONV_W), halo_prev, halo_next, tok(2 * D_MODEL), tok(D_MODEL),
                  pl.BlockSpec((1, 6, D_MODEL), mod_map), full(nfw), full(conv_w),
                  *[full(w) for w in weights], full(fnw)],
        out_specs=tok(D_MODEL),
        out_shape=jax.ShapeDtypeStruct((bsz, t, D_MODEL), F32),
        compiler_params=_params(2),
        name="merge_ffn",
    )(yr, cb, u, u, u, gt, x, mod, nfw, conv_w, *weights, fnw)


def _rope_tables(t, k_scale):
    pos = jnp.arange(t)
    rows = (pos // GRID_W).astype(F32)
    cols = (pos % GRID_W).astype(F32)
    freqs = ROPE_THETA ** (-jnp.arange(0, 64, 2, dtype=F32) / 64)
    ang = jnp.concatenate([rows[:, None] * freqs[None, :]] * 2 + [cols[:, None] * freqs[None, :]] * 2, axis=1)
    sign = jnp.tile(jnp.concatenate([-jnp.ones((32,), F32), jnp.ones((32,), F32)]), 2)
    cos, sin = jnp.cos(ang), jnp.sin(ang) * sign[None, :]
    return cos, sin, cos * k_scale, sin * k_scale


def _plain_tables(t, k_scale):
    one = jnp.ones((t, DK), F32)
    zero = jnp.zeros((t, DK), F32)
    return one, zero, one * k_scale, zero


def kernel(x, c, ctx, c_ctx, norm_mix_w, ada_w, ada_b, w_in, ret_decay_logit, conv_w, w_ret_out,
           w_conv_out, w_o, norm_ffn_w, w_ffn_in, w_ffn_out, final_norm_w):
    bsz, seq, _ = x.shape
    t_ctx = ctx.shape[1]
    depth = w_in.shape[0]
    k_scale = DK ** -0.5
    tm_lat = 512
    tm_ctx = 256
    assert seq % tm_lat == 0 and t_ctx % tm_ctx == 0 and bsz < MOD_ROWS

    cvec = jnp.zeros((MOD_ROWS, D_MODEL), F32).at[:bsz].set(c).at[bsz].set(c_ctx)
    mod = _modulation(cvec, ada_w, ada_b).reshape(depth, MOD_ROWS, 6, D_MODEL)
    lat_tables = _rope_tables(seq, k_scale)
    ctx_tables = _plain_tables(t_ctx, k_scale)

    xc = ctx
    for l in range(depth):
        last = l == depth - 1
        w_in_l = w_in[l].astype(BF16)
        wts = (w_ret_out[l].astype(BF16), w_conv_out[l].astype(BF16), w_o[l].astype(BF16),
               w_ffn_in[l].astype(BF16), w_ffn_out[l].astype(BF16))
        dl = jnp.broadcast_to(ret_decay_logit[l][:, :, None, None], (2, HEADS, 8, DV))
        mod_lat = mod[l, :bsz]
        mod_ctx = mod[l, bsz:bsz + 1]

        if last:
            state = _context_state(xc, mod_ctx, norm_mix_w[l], w_in_l[:, OFF_K:OFF_G], dl, k_scale)
        else:
            qc, kc, vc, sgc, cbc, uc, gtc = _in_projection(xc, mod_ctx, False, norm_mix_w[l], w_in_l,
                                                           ctx_tables, tm_ctx)
            yr_c, state = _retention(dl, qc, kc, vc, sgc, None, True, HEADS)
        q, k, v, sg, cb, u, gt = _in_projection(x, mod_lat, True, norm_mix_w[l], w_in_l, lat_tables, tm_lat)
        yr = _retention(dl, q, k, v, sg, state, False, 2)[0]
        x = _merge_ffn(yr, cb, u, gt, x, mod_lat, True, norm_ffn_w[l], conv_w[l], *wts, final_norm_w,
                       last, tm_lat)
        if not last:
            xc = _merge_ffn(yr_c, cbc, uc, gtc, xc, mod_ctx, False, norm_ffn_w[l], conv_w[l], *wts,
                            final_norm_w, False, tm_ctx)
    return x
```

```python
import functools

import jax
import jax.numpy as jnp
from jax import lax
from jax.experimental import pallas as pl
from jax.experimental.pallas import tpu as pltpu

F32 = jnp.float32
BF16 = jnp.bfloat16

D_MODEL = 1024
HEADS = 8
DK = 128
DV = 256
QK = HEADS * DK
VW = HEADS * DV
CONV_W = D_MODEL
D_FF = 2816
GRID_W = 64
ROPE_THETA = 10000.0
EPS = 1e-6
OFF_Q, OFF_K, OFF_V, OFF_G = 0, QK, 2 * QK, 2 * QK + VW
OFF_CB = 2 * QK + 2 * VW
OFF_CC = OFF_CB + CONV_W
OFF_CU = OFF_CC + CONV_W
OFF_GT = OFF_CU + CONV_W
IN_COLS = OFF_GT + 2 * D_MODEL

RET_CHUNK = 256
MOD_ROWS = 40
SUB_ROWS = 256
HALO = 16
VMEM_LIMIT = 56 * 1024 * 1024


def _silu(x):
    return x / (1.0 + jnp.exp(-x))


def _sigmoid(x):
    return 1.0 / (1.0 + jnp.exp(-x))


def _log_sigmoid(x):
    return jnp.minimum(x, 0.0) - jnp.log1p(jnp.exp(-jnp.abs(x)))


def _rms_norm(x, w):
    ms = jnp.mean(x * x, axis=-1, keepdims=True)
    return x * lax.rsqrt(ms + EPS) * w


def _dot(a, b):
    return jnp.dot(a, b, preferred_element_type=F32)


def _dot_t(a, b):
    return lax.dot_general(a, b, (((0,), (0,)), ((), ())), preferred_element_type=F32)


def _layer_spec(w, layer):
    return pl.BlockSpec((1,) + w.shape[1:], lambda *_: (layer, 0, 0), pipeline_mode=pl.Buffered(1))


def _params(n_axes):
    return pltpu.CompilerParams(dimension_semantics=("arbitrary",) * n_axes,
                                vmem_limit_bytes=VMEM_LIMIT)


def _mod_kernel(c_ref, w_ref, b_ref, o_ref):
    s = _silu(c_ref[...]).astype(BF16)
    o_ref[0] = _dot(s, w_ref[0].astype(BF16)) + b_ref[0]


def _modulation(cvec, ada_w, ada_b):
    depth = ada_w.shape[0]
    nj = ada_w.shape[2] // D_MODEL
    return pl.pallas_call(
        _mod_kernel,
        grid=(depth, nj),
        in_specs=[
            pl.BlockSpec((MOD_ROWS, D_MODEL), lambda l, j: (0, 0)),
            pl.BlockSpec((1, D_MODEL, D_MODEL), lambda l, j: (l, 0, j)),
            pl.BlockSpec((1, 1, D_MODEL), lambda l, j: (l, 0, j)),
        ],
        out_specs=pl.BlockSpec((1, MOD_ROWS, D_MODEL), lambda l, j: (l, 0, j)),
        out_shape=jax.ShapeDtypeStruct((depth, MOD_ROWS, ada_w.shape[2]), F32),
        compiler_params=_params(2),
        name="modulation",
    )(cvec, ada_w, ada_b.reshape(depth, 1, -1))


def _inproj_kernel(x_ref, mod_ref, nw_ref, w_ref, cq_ref, sq_ref, ck_ref, sk_ref,
                   q_ref, kt_ref, v_ref, sg_ref, cb_ref, u_ref, gt_ref):
    tm = x_ref.shape[1]
    sub = min(tm, SUB_ROWS)
    lane = lax.broadcasted_iota(jnp.int32, (sub, DK), 1)
    first_half = (lane % 64) < 32

    subs = [slice(r0, r0 + sub) for r0 in range(0, tm, sub)]
    hs = []
    for rs in subs:
        y = _rms_norm(x_ref[0, rs, :], nw_ref[...])
        hs.append((y * (1.0 + mod_ref[0, 1:2, :]) + mod_ref[0, 0:1, :]).astype(BF16))

    def proj(h, off, width):
        return _dot(h, w_ref[0, :, off:off + width])

    def rope(r, c_ref, s_ref, o_ref, rs, transposed):
        c = c_ref[rs, :]
        s = s_ref[rs, :]
        for hd in range(HEADS):
            xh = r[:, hd * DK:(hd + 1) * DK]
            partner = jnp.where(first_half, pltpu.roll(xh, DK - 32, 1), pltpu.roll(xh, 32, 1))
            out = xh * c + partner * s
            if transposed:
                o_ref[0, hd, :, rs] = out.T.astype(BF16)
            else:
                o_ref[0, rs, hd * DK:(hd + 1) * DK] = out.astype(BF16)

    for h, rs in zip(hs, subs):
        rope(proj(h, OFF_Q, QK), cq_ref, sq_ref, q_ref, rs, False)
    for h, rs in zip(hs, subs):
        rope(proj(h, OFF_K, QK), ck_ref, sk_ref, kt_ref, rs, True)
    for j in range(VW // D_MODEL):
        cols = slice(j * D_MODEL, (j + 1) * D_MODEL)
        for h, rs in zip(hs, subs):
            v_ref[0, rs, cols] = proj(h, OFF_V + j * D_MODEL, D_MODEL).astype(BF16)
        for h, rs in zip(hs, subs):
            sg_ref[0, rs, cols] = _silu(proj(h, OFF_G + j * D_MODEL, D_MODEL)).astype(BF16)
        for h, rs in zip(hs, subs):
            gt_ref[0, rs, cols] = _sigmoid(proj(h, OFF_GT + j * D_MODEL, D_MODEL)).astype(BF16)
    for h, rs in zip(hs, subs):
        cb_ref[0, rs, :] = proj(h, OFF_CB, CONV_W).astype(BF16)
    for h, rs in zip(hs, subs):
        u_ref[0, rs, :] = (proj(h, OFF_CC, CONV_W) * proj(h, OFF_CU, CONV_W)).astype(BF16)


def _in_projection(x, mod, per_batch_mod, norm_w, w_in, layer, tables, tm):
    bsz, t, _ = x.shape
    mod_map = (lambda b, i: (b, 0, 0)) if per_batch_mod else (lambda b, i: (0, 0, 0))
    tab_spec = pl.BlockSpec((tm, DK), lambda b, i: (i, 0))
    tok = lambda w: pl.BlockSpec((1, tm, w), lambda b, i: (b, i, 0))
    widths = (QK, None, VW, VW, CONV_W, CONV_W, 2 * D_MODEL)
    kt_spec = pl.BlockSpec((1, HEADS, DK, tm), lambda b, i: (b, 0, 0, i))
    kt_shape = jax.ShapeDtypeStruct((bsz, HEADS, DK, t), BF16)
    return pl.pallas_call(
        _inproj_kernel,
        grid=(bsz, t // tm),
        in_specs=[
            tok(D_MODEL),
            pl.BlockSpec((1, 6, D_MODEL), mod_map),
            pl.BlockSpec((1, D_MODEL), lambda b, i: (0, 0)),
            _layer_spec(w_in, layer),
            tab_spec, tab_spec, tab_spec, tab_spec,
        ],
        out_specs=[kt_spec if w is None else tok(w) for w in widths],
        out_shape=[kt_shape if w is None else jax.ShapeDtypeStruct((bsz, t, w), BF16) for w in widths],
        compiler_params=_params(2),
        name="in_projection",
    )(x, mod, norm_w.reshape(1, D_MODEL), w_in, *tables)


def _key_decays(lgf, lgb):
    ri = lax.broadcasted_iota(jnp.int32, (RET_CHUNK, DK), 0).astype(F32)
    return jnp.exp((RET_CHUNK - 1.0 - ri) * lgf), jnp.exp(ri * lgb)


def _key_decays_t(lgf, lgb):
    ci = lax.broadcasted_iota(jnp.int32, (DK, RET_CHUNK), 1).astype(F32)
    return jnp.exp((RET_CHUNK - 1.0 - ci) * lgf), jnp.exp(ci * lgb)


def _ret_kernel(*refs, n_chunks, n_heads, has_init, emit_state):
    refs = list(refs)
    dl_ref, q_ref, kt_ref, v_ref, sg_ref = refs[:5]
    pos = 5
    s0_ref = None
    if has_init:
        s0_ref = refs[pos]
        pos += 1
    yr_ref = refs[pos]
    pos += 1
    sfin_ref = None
    if emit_state:
        sfin_ref = refs[pos]
        pos += 1
    mask_ref, dec_ref, kdec_ref, cd_ref, u_ref, ss_ref = refs[pos:]
    c_len = RET_CHUNK

    @pl.when(pl.program_id(1) == 0)
    def _():
        ii = lax.broadcasted_iota(jnp.int32, (c_len, c_len), 0).astype(F32)
        jj = lax.broadcasted_iota(jnp.int32, (c_len, c_len), 1).astype(F32)
        diff = ii - jj
        ri = lax.broadcasted_iota(jnp.int32, (c_len, DK), 0).astype(F32)
        for hd in range(n_heads):
            lgf = _log_sigmoid(dl_ref[0, hd, 0:1, :])
            lgb = _log_sigmoid(dl_ref[1, hd, 0:1, :])
            mask_ref[hd] = (jnp.where(diff >= 0, jnp.exp(jnp.maximum(diff, 0.0) * lgf), 0.0)
                            + jnp.where(diff <= 0, jnp.exp(jnp.maximum(-diff, 0.0) * lgb), 0.0))
            dec_ref[hd, 0] = jnp.exp((ri + 1.0) * lgf[:, :DK]).astype(BF16)
            dec_ref[hd, 1] = jnp.exp((c_len - ri) * lgb[:, :DK]).astype(BF16)
            k_dec_f, k_dec_b = _key_decays_t(lgf, lgb)
            kdec_ref[hd, 0] = k_dec_f.astype(BF16)
            kdec_ref[hd, 1] = k_dec_b.astype(BF16)
            cd_ref[hd, 0] = jnp.broadcast_to(jnp.exp(c_len * lgf), (8, DV))
            cd_ref[hd, 1] = jnp.broadcast_to(jnp.exp(c_len * lgb), (8, DV))

    rows = [slice(c * c_len, (c + 1) * c_len) for c in range(n_chunks)]

    def state_pieces(hd):
        vcols = slice(hd * DV, (hd + 1) * DV)
        carry = {}

        def inc(c):
            kt = kt_ref[0, hd, :, rows[c]]
            kk = jnp.concatenate([kt * kdec_ref[hd, 0], kt * kdec_ref[hd, 1]], axis=0)
            u_ref[hd, c] = _dot(kk, v_ref[0, rows[c], vcols])

        def scan_step(c, d, first, last):
            half = slice(d * DK, (d + 1) * DK)
            if first:
                carry[d] = s0_ref[0, hd, half, :] if has_init else jnp.zeros((DK, DV), F32)
            ss_ref[hd, c, half, :] = carry[d].astype(BF16)
            carry[d] = carry[d] * cd_ref[hd, d, 0:1, :] + u_ref[hd, c, half, :]
            if last and emit_state:
                sfin_ref[0, hd, half, :] = carry[d]

        pieces = [functools.partial(inc, c) for c in range(n_chunks)]
        pieces += [functools.partial(scan_step, c, 0, c == 0, c == n_chunks - 1) for c in range(n_chunks)]
        pieces += [functools.partial(scan_step, c, 1, c == n_chunks - 1, c == 0)
                   for c in reversed(range(n_chunks))]
        return pieces

    def out_chunk(hd, c):
        kcols = slice(hd * DK, (hd + 1) * DK)
        vcols = slice(hd * DV, (hd + 1) * DV)
        qc = q_ref[0, rows[c], kcols]
        scores = _dot(qc, kt_ref[0, hd, :, rows[c]])
        p = (scores * mask_ref[hd]).astype(BF16)
        qq = jnp.concatenate([qc * dec_ref[hd, 0], qc * dec_ref[hd, 1]], axis=1)
        o = _dot(p, v_ref[0, rows[c], vcols]) + _dot(qq, ss_ref[hd, c])
        mu = jnp.mean(o, axis=-1, keepdims=True)
        oc = o - mu
        var = jnp.mean(oc * oc, axis=-1, keepdims=True)
        on = oc * lax.rsqrt(var + EPS)
        yr_ref[0, rows[c], vcols] = on.astype(BF16) * sg_ref[0, rows[c], vcols]

    for piece in state_pieces(0):
        piece()
    for hd in range(n_heads):
        pending = state_pieces(hd + 1) if hd + 1 < n_heads else []
        per_chunk = -(-len(pending) // n_chunks)
        for c in range(n_chunks):
            out_chunk(hd, c)
            for piece in pending[c * per_chunk:(c + 1) * per_chunk]:
                piece()


def _retention(dl, q, kt, v, sg, s0, emit_state, n_heads):
    bsz, t, _ = q.shape
    n_chunks = t // RET_CHUNK
    has_init = s0 is not None
    tok = lambda w: pl.BlockSpec((1, t, n_heads * w), lambda g, b: (b, 0, g))
    st = pl.BlockSpec((1, n_heads, 2 * DK, DV), lambda g, b: (b, g, 0, 0))
    out_specs = [tok(DV)]
    out_shape = [jax.ShapeDtypeStruct((bsz, t, VW), BF16)]
    if emit_state:
        out_specs.append(st)
        out_shape.append(jax.ShapeDtypeStruct((bsz, HEADS, 2 * DK, DV), F32))
    args = (dl, q, kt, v, sg) + ((s0,) if has_init else ())
    return pl.pallas_call(
        functools.partial(_ret_kernel, n_chunks=n_chunks, n_heads=n_heads, has_init=has_init,
                          emit_state=emit_state),
        grid=(HEADS // n_heads, bsz),
        in_specs=[pl.BlockSpec((2, n_heads, 8, DV), lambda g, b: (0, g, 0, 0)),
                  tok(DK), pl.BlockSpec((1, n_heads, DK, t), lambda g, b: (b, g, 0, 0)),
                  tok(DV), tok(DV)] + ([st] if has_init else []),
        out_specs=out_specs,
        out_shape=out_shape,
        scratch_shapes=[pltpu.VMEM((n_heads, RET_CHUNK, RET_CHUNK), F32),
                        pltpu.VMEM((n_heads, 2, RET_CHUNK, DK), BF16),
                        pltpu.VMEM((n_heads, 2, DK, RET_CHUNK), BF16),
                        pltpu.VMEM((n_heads, 2, 8, DV), F32),
                        pltpu.VMEM((n_heads, n_chunks, 2 * DK, DV), F32),
                        pltpu.VMEM((n_heads, n_chunks, 2 * DK, DV), BF16)],
        compiler_params=_params(2),
        name="retention",
    )(*args)


def _ctx_state_kernel(x_ref, mod_ref, nw_ref, wk_ref, wv_ref, dl_ref, st_ref, *, n_chunks, k_scale):
    c_len = RET_CHUNK
    y = _rms_norm(x_ref[0], nw_ref[...])
    h = (y * (1.0 + mod_ref[0, 1:2, :]) + mod_ref[0, 0:1, :]).astype(BF16)
    incs = []
    for c in range(n_chunks):
        hc = h[c * c_len:(c + 1) * c_len, :]
        incs.append((_dot(hc, wk_ref[0]) * k_scale, _dot(hc, wv_ref[0]).astype(BF16)))
    for hd in range(HEADS):
        lgf = _log_sigmoid(dl_ref[0, hd, 0:1, :])
        lgb = _log_sigmoid(dl_ref[1, hd, 0:1, :])
        k_dec_f, k_dec_b = _key_decays(lgf[:, :DK], lgb[:, :DK])
        chunk_dec_f = jnp.exp(c_len * lgf)
        chunk_dec_b = jnp.exp(c_len * lgb)
        u = []
        for kf, vb in incs:
            kh = kf[:, hd * DK:(hd + 1) * DK]
            kk = jnp.concatenate([kh * k_dec_f, kh * k_dec_b], axis=1).astype(BF16)
            u.append(_dot_t(kk, vb[:, hd * DV:(hd + 1) * DV]))
        s_f = u[0][0:DK, :]
        for c in range(1, n_chunks):
            s_f = s_f * chunk_dec_f + u[c][0:DK, :]
        s_b = u[n_chunks - 1][DK:2 * DK, :]
        for c in reversed(range(n_chunks - 1)):
            s_b = s_b * chunk_dec_b + u[c][DK:2 * DK, :]
        st_ref[0, hd, 0:DK, :] = s_f
        st_ref[0, hd, DK:2 * DK, :] = s_b


def _context_state(xc, mod, norm_w, w_in, layer, dl, k_scale):
    bsz, t, _ = xc.shape
    return pl.pallas_call(
        functools.partial(_ctx_state_kernel, n_chunks=t // RET_CHUNK, k_scale=k_scale),
        grid=(bsz,),
        in_specs=[
            pl.BlockSpec((1, t, D_MODEL), lambda b: (b, 0, 0)),
            pl.BlockSpec((1, 6, D_MODEL), lambda b: (0, 0, 0)),
            pl.BlockSpec((1, D_MODEL), lambda b: (0, 0)),
            pl.BlockSpec((1, D_MODEL, QK), lambda b: (layer, 0, OFF_K // QK), pipeline_mode=pl.Buffered(1)),
            pl.BlockSpec((1, D_MODEL, VW), lambda b: (layer, 0, OFF_V // VW), pipeline_mode=pl.Buffered(1)),
            pl.BlockSpec((2, HEADS, 8, DV), lambda b: (0, 0, 0, 0)),
        ],
        out_specs=pl.BlockSpec((1, HEADS, 2 * DK, DV), lambda b: (b, 0, 0, 0)),
        out_shape=jax.ShapeDtypeStruct((bsz, HEADS, 2 * DK, DV), F32),
        compiler_params=_params(1),
        name="context_state",
    )(xc, mod, norm_w.reshape(1, D_MODEL), w_in, w_in, dl)


FF_CHUNKS = ((0, 1024), (1024, 2048), (2048, D_FF))


def _merge_ffn_kernel(yr_ref, cb_ref, u_ref, up_ref, un_ref, gt_ref, x_ref, mod_ref, nfw_ref, cw_ref,
                      wro_ref, wco_ref, wo_ref, wfi_ref, wfo_ref, fnw_ref, o_ref, *, final_norm):
    tm = x_ref.shape[1]
    i = pl.program_id(1)
    sub = min(tm, SUB_ROWS)
    subs = [slice(r0, r0 + sub) for r0 in range(0, tm, sub)]
    has_prev = jnp.where(i > 0, 1.0, 0.0)
    has_next = jnp.where(i < pl.num_programs(1) - 1, 1.0, 0.0)
    pos = lax.broadcasted_iota(jnp.int32, (sub, CONV_W), 0)

    def gated_conv(rs):
        r0, r1 = rs.start, rs.stop
        u = u_ref[0, rs, :].astype(F32)
        if r0 == 0:
            head_row = up_ref[0].astype(F32)[HALO - 1:HALO, :] * has_prev
        else:
            head_row = u_ref[0, r0 - HALO:r0, :].astype(F32)[HALO - 1:HALO, :]
        if r1 == tm:
            tail_row = un_ref[0].astype(F32)[0:1, :] * has_next
        else:
            tail_row = u_ref[0, r1:r1 + HALO, :].astype(F32)[0:1, :]
        prev = jnp.where(pos == 0, head_row, pltpu.roll(u, 1, 0))
        nxt = jnp.where(pos == sub - 1, tail_row, pltpu.roll(u, sub - 1, 0))
        conv = prev * cw_ref[0:1, :] + u * cw_ref[1:2, :] + nxt * cw_ref[2:3, :]
        return (cb_ref[0, rs, :].astype(F32) * conv).astype(BF16)

    y_ret, y_conv = [], []
    z_next = gated_conv(subs[0])
    for n, rs in enumerate(subs):
        z = z_next
        y_ret.append(_dot(yr_ref[0, rs, :], wro_ref[0]))
        if n + 1 < len(subs):
            z_next = gated_conv(subs[n + 1])
        y_conv.append(_dot(z, wco_ref[0]))
    mix = [(gt_ref[0, rs, 0:D_MODEL].astype(F32) * yr + gt_ref[0, rs, D_MODEL:2 * D_MODEL].astype(F32) * yc
            ).astype(BF16) for rs, yr, yc in zip(subs, y_ret, y_conv)]
    x1 = [x_ref[0, rs, :] + mod_ref[0, 2:3, :] * _dot(m, wo_ref[0]) for rs, m in zip(subs, mix)]
    hs = [(_rms_norm(xx, nfw_ref[...]) * (1.0 + mod_ref[0, 4:5, :]) + mod_ref[0, 3:4, :]).astype(BF16)
          for xx in x1]
    acc = [None] * len(subs)
    for c0, c1 in FF_CHUNKS:
        for n, h in enumerate(hs):
            a = (_silu(_dot(h, wfi_ref[0, :, c0:c1])) * _dot(h, wfi_ref[0, :, D_FF + c0:D_FF + c1])).astype(BF16)
            part = _dot(a, wfo_ref[0, c0:c1, :])
            acc[n] = part if acc[n] is None else acc[n] + part
    for rs, xx, ff in zip(subs, x1, acc):
        x2 = xx + mod_ref[0, 5:6, :] * ff
        if final_norm:
            x2 = _rms_norm(x2, fnw_ref[...])
        o_ref[0, rs, :] = x2


def _merge_ffn(yr, cb, u, gt, x, mod, per_batch_mod, norm_ffn_w, conv_w, weights, layer, final_norm_w,
               final_norm, tm):
    bsz, t, _ = x.shape
    mod_map = (lambda b, i: (b, 0, 0)) if per_batch_mod else (lambda b, i: (0, 0, 0))
    tok = lambda w: pl.BlockSpec((1, tm, w), lambda b, i: (b, i, 0))
    full = lambda a: pl.BlockSpec(a.shape, lambda b, i: (0,) * a.ndim)
    per_tile = tm // HALO
    last_halo = t // HALO - 1
    halo_prev = pl.BlockSpec((1, HALO, CONV_W), lambda b, i: (b, jnp.maximum(i * per_tile - 1, 0), 0))
    halo_next = pl.BlockSpec((1, HALO, CONV_W), lambda b, i: (b, jnp.minimum((i + 1) * per_tile, last_halo), 0))
    nfw = norm_ffn_w.reshape(1, D_MODEL)
    fnw = final_norm_w.reshape(1, D_MODEL)
    return pl.pallas_call(
        functools.partial(_merge_ffn_kernel, final_norm=final_norm),
        grid=(bsz, t // tm),
        in_specs=[tok(VW), tok(CONV_W), tok(CONV_W), halo_prev, halo_next, tok(2 * D_MODEL), tok(D_MODEL),
                  pl.BlockSpec((1, 6, D_MODEL), mod_map), full(nfw), full(conv_w),
                  *[_layer_spec(w, layer) for w in weights], full(fnw)],
        out_specs=tok(D_MODEL),
        out_shape=jax.ShapeDtypeStruct((bsz, t, D_MODEL), F32),
        compiler_params=_params(2),
        name="merge_ffn",
    )(yr, cb, u, u, u, gt, x, mod, nfw, conv_w, *weights, fnw)


def _rope_tables(t, k_scale):
    pos = jnp.arange(t)
    rows = (pos // GRID_W).astype(F32)
    cols = (pos % GRID_W).astype(F32)
    freqs = ROPE_THETA ** (-jnp.arange(0, 64, 2, dtype=F32) / 64)
    ang = jnp.concatenate([rows[:, None] * freqs[None, :]] * 2 + [cols[:, None] * freqs[None, :]] * 2, axis=1)
    sign = jnp.tile(jnp.concatenate([-jnp.ones((32,), F32), jnp.ones((32,), F32)]), 2)
    cos, sin = jnp.cos(ang), jnp.sin(ang) * sign[None, :]
    return cos, sin, cos * k_scale, sin * k_scale


def _plain_tables(t, k_scale):
    one = jnp.ones((t, DK), F32)
    zero = jnp.zeros((t, DK), F32)
    return one, zero, one * k_scale, zero


def kernel(x, c, ctx, c_ctx, norm_mix_w, ada_w, ada_b, w_in, ret_decay_logit, conv_w, w_ret_out,
           w_conv_out, w_o, norm_ffn_w, w_ffn_in, w_ffn_out, final_norm_w):
    bsz, seq, _ = x.shape
    t_ctx = ctx.shape[1]
    depth = w_in.shape[0]
    k_scale = DK ** -0.5
    tm_lat = 512
    tm_ctx = 256
    assert seq % tm_lat == 0 and t_ctx % tm_ctx == 0 and bsz < MOD_ROWS

    cvec = jnp.zeros((MOD_ROWS, D_MODEL), F32).at[:bsz].set(c).at[bsz].set(c_ctx)
    mod = _modulation(cvec, ada_w, ada_b).reshape(depth, MOD_ROWS, 6, D_MODEL)
    lat_tables = _rope_tables(seq, k_scale)
    ctx_tables = _plain_tables(t_ctx, k_scale)

    w_in_bf = w_in.astype(BF16)
    wts = tuple(w.astype(BF16) for w in (w_ret_out, w_conv_out, w_o, w_ffn_in, w_ffn_out))

    xc = ctx
    for l in range(depth):
        last = l == depth - 1
        dl = jnp.broadcast_to(ret_decay_logit[l][:, :, None, None], (2, HEADS, 8, DV))
        mod_lat = mod[l, :bsz]
        mod_ctx = mod[l, bsz:bsz + 1]

        if last:
            state = _context_state(xc, mod_ctx, norm_mix_w[l], w_in_bf, l, dl, k_scale)
        else:
            qc, kc, vc, sgc, cbc, uc, gtc = _in_projection(xc, mod_ctx, False, norm_mix_w[l], w_in_bf, l,
                                                           ctx_tables, tm_ctx)
            yr_c, state = _retention(dl, qc, kc, vc, sgc, None, True, HEADS)
        q, k, v, sg, cb, u, gt = _in_projection(x, mod_lat, True, norm_mix_w[l], w_in_bf, l, lat_tables, tm_lat)
        yr = _retention(dl, q, k, v, sg, state, False, 2)[0]
        x = _merge_ffn(yr, cb, u, gt, x, mod_lat, True, norm_ffn_w[l], conv_w[l], wts, l, final_norm_w,
                       last, tm_lat)
        if not last:
            xc = _merge_ffn(yr_c, cbc, uc, gtc, xc, mod_ctx, False, norm_ffn_w[l], conv_w[l], wts, l,
                            final_norm_w, False, tm_ctx)
    return x
```
